```python
import math
import jax, jax.numpy as jnp
from jax import lax
import numpy as np

D_MODEL = 1024
BATCH = 2
SEQ = 8192
DEPTH = 4
DEC_BATCH = 128
DEC_SEQ = 1
PAST_LEN = 2048
PAGE_SIZE = 128

A_HEADS = 4
A_QK_DIM = 64
A_V_DIM = 2 * A_QK_DIM
A_WIDTH = A_HEADS * A_V_DIM
P_WINDOWS = (2, 4, 8, 16)
P_GROUPS = 4
P_GROUP_DIM = 128
P_WIDTH = P_GROUPS * P_GROUP_DIM
POOL_BUF = 15
H_HEADS = 4
H_DK = 128
H_DV = 128
H_FWIDTH = H_HEADS * H_DK
H_WIDTH = H_HEADS * H_DV
H_CHUNK = 64
N_BRANCH = 3
BRANCH_WIDTH = 512
D_FF = 4 * D_MODEL
REL_BUCKETS = 32
REL_MAX_DIST = 128
Q_BLOCK = 128
EPS = 1e-6
SPLITS = (A_HEADS * 2 * A_QK_DIM, A_HEADS * 2 * A_QK_DIM, A_WIDTH, P_WIDTH,
          H_FWIDTH, H_FWIDTH, H_WIDTH, H_WIDTH, N_BRANCH * D_MODEL)
IN_WIDTH = (4 * A_HEADS * A_QK_DIM + A_WIDTH + P_WIDTH + 2 * H_FWIDTH + 2 * H_WIDTH
            + N_BRANCH * D_MODEL)

kernel_name = 'hybrid_diffattn_pool_hgrn2_decoder_step'


def split_points():
    pts, acc = [], 0
    for s in SPLITS[:-1]:
        acc += s
        pts.append(acc)
    return pts


def rmsnorm(x, g):
    xf = x.astype(jnp.float32)
    y = xf * lax.rsqrt(jnp.mean(xf * xf, axis=-1, keepdims=True) + EPS)
    return (y * g.astype(jnp.float32)).astype(x.dtype)


def rel_bucket(q_pos, k_pos):
    n = jnp.maximum(q_pos[:, None] - k_pos[None, :], 0)
    max_exact = REL_BUCKETS // 2
    nf = jnp.maximum(n, max_exact).astype(jnp.float32)
    large = max_exact + (jnp.log(nf / max_exact) / math.log(REL_MAX_DIST / max_exact)
                         * (REL_BUCKETS - max_exact)).astype(jnp.int32)
    large = jnp.minimum(large, REL_BUCKETS - 1)
    return jnp.where(n < max_exact, n, large)


def diff_attend(q, k, v, q_pos, k_pos, lam, rel_table):
    s = jnp.einsum('bqhmd,bkhmd->bhmqk', q, k).astype(jnp.float32) * (A_QK_DIM ** -0.5)
    bias = rel_table.astype(jnp.float32)[rel_bucket(q_pos, k_pos)]
    s = s + jnp.transpose(bias, (2, 0, 1))[None, :, None]
    mask = k_pos[None, :] <= q_pos[:, None]
    p = jax.nn.softmax(jnp.where(mask, s, -1e30), axis=-1)
    w = p[:, :, 0] - lam * p[:, :, 1]
    return jnp.einsum('bhqk,bkhd->bqhd', w.astype(v.dtype), v)


def pool_mix(u_ext, first_pos, w_pool, p_scale):
    B, L, _ = u_ext.shape
    T = L - POOL_BUF
    uf = u_ext.astype(jnp.float32)
    cs = jnp.cumsum(uf, axis=1)
    cs = jnp.concatenate([jnp.zeros_like(cs[:, :1]), cs], axis=1)
    end = cs[:, POOL_BUF + 1:]
    pos = first_pos + jnp.arange(T)
    outs = []
    for g, win in enumerate(P_WINDOWS):
        sl = slice(g * P_GROUP_DIM, (g + 1) * P_GROUP_DIM)
        start = cs[:, POOL_BUF + 1 - win: POOL_BUF + 1 - win + T, sl]
        cnt = jnp.minimum(pos + 1, win).astype(jnp.float32)[None, :, None]
        outs.append((end[..., sl] - start) / cnt - uf[:, POOL_BUF:, sl])
    pooled = jnp.stack(outs, axis=2)
    mixed = jnp.einsum('btgc,gcd->btgd', pooled, w_pool.astype(jnp.float32))
    return mixed.reshape(B, T, P_WIDTH) * p_scale.astype(jnp.float32)


def hgrn_chunk(S, inp):
    q, k, v, lg = inp
    C = q.shape[2]
    b = jnp.cumsum(lg, axis=2)
    causal = jnp.tril(jnp.ones((C, C), dtype=bool))
    expo = jnp.where(causal[None, None, :, :, None],
                     b[:, :, :, None, :] - b[:, :, None, :, :], -jnp.inf)
    A = jnp.einsum('bhtk,bhtsk,bhsk->bhts', q, jnp.exp(expo), k)
    o = jnp.einsum('bhts,bhsv->bhtv', A, v) + jnp.einsum('bhtk,bhkv->bhtv', q * jnp.exp(b), S)
    k_end = k * jnp.exp(b[:, :, -1:, :] - b)
    S_new = jnp.exp(b[:, :, -1])[..., None] * S + jnp.einsum('bhsk,bhsv->bhkv', k_end, v)
    return S_new, o


def hgrn_scan(q, k, v, log_g, S0):
    B, T = q.shape[:2]
    chunk = H_CHUNK if T % H_CHUNK == 0 else T
    n = T // chunk

    def to_chunks(a):
        a = a.astype(jnp.float32).reshape(B, n, chunk, H_HEADS, a.shape[-1])
        return jnp.transpose(a, (1, 0, 3, 2, 4))

    S_fin, o = lax.scan(hgrn_chunk, S0.astype(jnp.float32),
                        (to_chunks(q), to_chunks(k), to_chunks(v), to_chunks(log_g)))
    o = jnp.transpose(o, (1, 0, 3, 2, 4)).reshape(B, T, H_HEADS, H_DV)
    return o, S_fin


def block(x, attend, pool_ctx, first_pos, S0, lb, lam_init, wl):
    (w_in, g_mix, g_q, g_k, lam_p, g_sub, w_pool, p_scale, g_h,
     w_branch, w_out, g_mlp, w_up, w_down) = wl
    B, T, _ = x.shape
    h = rmsnorm(x, g_mix)
    z = h @ w_in
    q, k, v, u, hq, hf, hi, hg, gz = jnp.split(z, split_points(), axis=-1)
    q = rmsnorm(q.reshape(B, T, A_HEADS, 2, A_QK_DIM), g_q)
    k = rmsnorm(k.reshape(B, T, A_HEADS, 2, A_QK_DIM), g_k)
    v = v.reshape(B, T, A_HEADS, A_V_DIM)
    lp = lam_p.astype(jnp.float32)
    lam = jnp.exp(jnp.sum(lp[0] * lp[1])) - jnp.exp(jnp.sum(lp[2] * lp[3])) + lam_init
    a = attend(q, k, v, lam)
    a = (rmsnorm(a, g_sub) * (1.0 - lam_init)).reshape(B, T, A_WIDTH).astype(x.dtype)
    u_ext = jnp.concatenate([pool_ctx.astype(u.dtype), u], axis=1)
    p = pool_mix(u_ext, first_pos, w_pool, p_scale).astype(x.dtype)
    hq = jax.nn.silu(hq.reshape(B, T, H_HEADS, H_DK))
    f = hf.reshape(B, T, H_HEADS, H_DK).astype(jnp.float32)
    log_g = jnp.logaddexp(jnp.log(lb), jnp.log1p(-lb) + jax.nn.log_sigmoid(f))
    kk = (1.0 - lb) * jax.nn.sigmoid(-f)
    o, S_new = hgrn_scan(hq, kk, hi.reshape(B, T, H_HEADS, H_DV), log_g, S0)
    o = rmsnorm(o, g_h) * jax.nn.silu(hg.reshape(B, T, H_HEADS, H_DV).astype(jnp.float32))
    o = o.reshape(B, T, H_WIDTH).astype(x.dtype)
    br = jnp.stack([a, p, o], axis=2)
    proj = jnp.einsum('btnc,ncd->btnd', br, w_branch)
    gates = jax.nn.sigmoid(gz.reshape(B, T, N_BRANCH, D_MODEL).astype(jnp.float32))
    merged = jnp.sum(gates * proj.astype(jnp.float32), axis=2).astype(x.dtype)
    x = x + merged @ w_out
    hm = rmsnorm(x, g_mlp)
    x = x + jnp.square(jax.nn.relu(hm @ w_up)) @ w_down
    return x, k, v, u_ext[:, -POOL_BUF:], S_new.astype(x.dtype)


def setup_inputs(seed: int = 0) -> dict:
    key = jax.random.key(seed)
    ks = jax.random.split(key, 24)
    f32 = jnp.float32
    n_pages = PAST_LEN // PAGE_SIZE
    n_used = DEC_BATCH * n_pages
    n_phys = n_used + n_used // 4

    def nrm(k, shape, scale):
        return jax.random.normal(k, shape, f32) * scale

    def gain(k, shape):
        return 1.0 + 0.01 * jax.random.normal(k, shape, f32)

    page_table = jax.random.permutation(ks[6], n_phys)[:n_used].reshape(DEC_BATCH, n_pages).astype(jnp.int32)
    return {
        'x_prompt': nrm(ks[0], (BATCH, SEQ, D_MODEL), 1.0),
        'x_sample': nrm(ks[1], (DEC_BATCH, DEC_SEQ, D_MODEL), 1.0),
        'cache_k': nrm(ks[2], (DEPTH, n_phys, PAGE_SIZE, A_HEADS, 2, A_QK_DIM), 1.0),
        'cache_v': nrm(ks[3], (DEPTH, n_phys, PAGE_SIZE, A_HEADS, A_V_DIM), 1.0),
        'state_pool': nrm(ks[4], (DEPTH, DEC_BATCH, POOL_BUF, P_WIDTH), 1.0),
        'state_hgrn': nrm(ks[5], (DEPTH, DEC_BATCH, H_HEADS, H_DK, H_DV), 0.5),
        'page_table': page_table,
        'rel_table': nrm(ks[7], (REL_BUCKETS, A_HEADS), 0.5),
        'lb_param': nrm(ks[8], (DEPTH, H_FWIDTH), 0.5),
        'w_in': nrm(ks[9], (DEPTH, D_MODEL, IN_WIDTH), D_MODEL ** -0.5),
        'g_mix': gain(ks[10], (DEPTH, D_MODEL)),
        'g_q': gain(ks[11], (DEPTH, A_QK_DIM)),
        'g_k': gain(ks[12], (DEPTH, A_QK_DIM)),
        'lam_p': nrm(ks[13], (DEPTH, 4, A_QK_DIM), 0.1),
        'g_sub': gain(ks[14], (DEPTH, A_V_DIM)),
        'w_pool': nrm(ks[15], (DEPTH, P_GROUPS, P_GROUP_DIM, P_GROUP_DIM), P_GROUP_DIM ** -0.5),
        'pool_scale': 1.0 + 0.1 * jax.random.normal(ks[16], (DEPTH, P_WIDTH), f32),
        'g_h': gain(ks[17], (DEPTH, H_DV)),
        'w_branch': nrm(ks[18], (DEPTH, N_BRANCH, BRANCH_WIDTH, D_MODEL), BRANCH_WIDTH ** -0.5),
        'w_out': nrm(ks[19], (DEPTH, D_MODEL, D_MODEL), D_MODEL ** -0.5),
        'g_mlp': gain(ks[20], (DEPTH, D_MODEL)),
        'w_up': nrm(ks[21], (DEPTH, D_MODEL, D_FF), D_MODEL ** -0.5),
        'w_down': nrm(ks[22], (DEPTH, D_FF, D_MODEL), D_FF ** -0.5),
    }


def reference(x_prompt, x_sample, cache_k, cache_v, state_pool, state_hgrn, page_table,
              rel_table, lb_param, w_in, g_mix, g_q, g_k, lam_p, g_sub, w_pool, pool_scale,
              g_h, w_branch, w_out, g_mlp, w_up, w_down):
    lb_all = jnp.cumsum(jax.nn.softmax(lb_param.astype(jnp.float32), axis=0), axis=0)
    lb_all = lb_all - lb_all[:1]
    xp, xs = x_prompt, x_sample
    Bp, Tp = xp.shape[:2]
    Bs, Ts = xs.shape[:2]
    kp_l, vp_l, ks_l, vs_l, pp_l, ps_l, sp_l, ss_l = [], [], [], [], [], [], [], []
    for l in range(DEPTH):
        lam_init = 0.8 - 0.6 * math.exp(-0.3 * l)
        lb = lb_all[l].reshape(H_HEADS, H_DK)
        wl = (w_in[l], g_mix[l], g_q[l], g_k[l], lam_p[l], g_sub[l], w_pool[l], pool_scale[l],
              g_h[l], w_branch[l], w_out[l], g_mlp[l], w_up[l], w_down[l])

        def attend_prompt(q, k, v, lam):
            B, T = q.shape[:2]
            nblk = T // Q_BLOCK
            kpos = jnp.arange(T)
            qb = jnp.moveaxis(q.reshape(B, nblk, Q_BLOCK, A_HEADS, 2, A_QK_DIM), 1, 0)
            starts = jnp.arange(nblk) * Q_BLOCK
            out = lax.map(lambda a: diff_attend(a[0], k, v, a[1] + jnp.arange(Q_BLOCK), kpos,
                                                lam, rel_table), (qb, starts))
            return jnp.moveaxis(out, 0, 1).reshape(B, T, A_HEADS, A_V_DIM)

        def attend_sample(q, k, v, lam, l=l):
            B, T = q.shape[:2]
            k_past = cache_k[l][page_table].reshape(B, PAST_LEN, A_HEADS, 2, A_QK_DIM)
            v_past = cache_v[l][page_table].reshape(B, PAST_LEN, A_HEADS, A_V_DIM)
            k_all = jnp.concatenate([k_past.astype(k.dtype), k], axis=1)
            v_all = jnp.concatenate([v_past.astype(v.dtype), v], axis=1)
            return diff_attend(q, k_all, v_all, PAST_LEN + jnp.arange(T),
                               jnp.arange(PAST_LEN + T), lam, rel_table)

        xp, kp, vp, pp, sp = block(xp, attend_prompt, jnp.zeros((Bp, POOL_BUF, P_WIDTH), xp.dtype), 0,
                                   jnp.zeros((Bp, H_HEADS, H_DK, H_DV), jnp.float32), lb, lam_init, wl)
        xs, ks, vs, ps, ss = block(xs, attend_sample, state_pool[l], PAST_LEN, state_hgrn[l],
                                   lb, lam_init, wl)
        kp_l.append(kp); vp_l.append(vp); pp_l.append(pp); sp_l.append(sp)
        ks_l.append(ks); vs_l.append(vs); ps_l.append(ps); ss_l.append(ss)
    return (xp, xs,
            jnp.stack(kp_l), jnp.stack(vp_l), jnp.stack(ks_l), jnp.stack(vs_l),
            jnp.stack(pp_l), jnp.stack(ps_l), jnp.stack(sp_l), jnp.stack(ss_l))
```

```python
import functools
import math

import numpy as np
import jax
import jax.numpy as jnp
from jax import lax
from jax.experimental import pallas as pl
from jax.experimental.pallas import tpu as pltpu

F32 = jnp.float32
BF16 = jnp.bfloat16
EPS = 1e-6

A_HEADS = 4
A_QK_DIM = 64
A_V_DIM = 128
HEAD_W = 128
P_WINDOWS = (2, 4, 8, 16)
POOL_BUF = 15
N_BRANCH = 3
BRANCH_W = 512
REL_BUCKETS = 32
REL_MAX_DIST = 128
PAGE = 128
NEG = -1e30

ZC_Q, ZC_K, ZC_V, ZC_U, ZC_HQ, ZC_HF, ZC_HI, ZC_HG = range(8)
ZC_GZ = 8 * 512

VMEM_LIMIT = 48 * 1024 * 1024
ATT_TILE = 512
HGRN_CHUNK = 64
HGRN_SUB = 8


def _cparams(sem):
    return pltpu.CompilerParams(dimension_semantics=sem, vmem_limit_bytes=VMEM_LIMIT)


def _resident(shape):
    nd = len(shape)
    return pl.BlockSpec(shape, lambda *_: (0,) * nd)


def _sigmoid(x):
    return 1.0 / (1.0 + jnp.exp(-x))


def _silu(x):
    return x * _sigmoid(x)


def _inproj_body(x_ref, g_ref, w_ref, z_ref, h_scr):
    @pl.when(pl.program_id(1) == 0)
    def _():
        x = x_ref[...]
        ms = jnp.mean(x * x, axis=-1, keepdims=True)
        h_scr[...] = (x * lax.rsqrt(ms + EPS) * g_ref[...]).astype(BF16)

    z_ref[...] = jnp.dot(h_scr[...], w_ref[...], preferred_element_type=F32)


def _inproj(x, g, w, tm, tn=1024):
    M, D = x.shape
    N = w.shape[1]
    return pl.pallas_call(
        _inproj_body,
        grid=(M // tm, N // tn),
        in_specs=[
            pl.BlockSpec((tm, D), lambda i, j: (i, 0)),
            pl.BlockSpec((1, D), lambda i, j: (0, 0)),
            pl.BlockSpec((D, tn), lambda i, j: (0, j)),
        ],
        out_specs=pl.BlockSpec((tm, tn), lambda i, j: (i, j)),
        out_shape=jax.ShapeDtypeStruct((M, N), F32),
        scratch_shapes=[pltpu.VMEM((tm, D), BF16)],
        compiler_params=_cparams(("parallel", "arbitrary")),
        name="inproj",
    )(x, g, w)


def _group_norm_T(xT, gcol):
    n = xT.shape[1]
    x3 = xT.reshape(8, A_QK_DIM, n)
    ms = jnp.mean(x3 * x3, axis=1, keepdims=True)
    return (x3 * lax.rsqrt(ms + EPS)).reshape(8 * A_QK_DIM, n) * gcol


def _kvprep_body(k_ref, v_ref, gk_ref, kT_ref, kTb_ref, v4_ref, vb_ref, *, tm):
    kn = _group_norm_T(k_ref[...].T, gk_ref[...])
    kT_ref[...] = kn
    kTb_ref[...] = kn.astype(BF16)
    v = v_ref[...]
    vb_ref[...] = v.astype(BF16)
    for h in range(A_HEADS):
        v4_ref[pl.ds(h, tm, stride=A_HEADS), :] = v[:, h * HEAD_W:(h + 1) * HEAD_W]


def _kvprep(z, gk_col, B, T, tm=512):
    nt = T // tm
    M = B * T
    return pl.pallas_call(
        functools.partial(_kvprep_body, tm=tm),
        grid=(B, nt),
        in_specs=[
            pl.BlockSpec((tm, 512), lambda b, i: (b * nt + i, ZC_K)),
            pl.BlockSpec((tm, 512), lambda b, i: (b * nt + i, ZC_V)),
            pl.BlockSpec((512, 1), lambda b, i: (0, 0)),
        ],
        out_specs=[
            pl.BlockSpec((None, 512, tm), lambda b, i: (b, 0, i)),
            pl.BlockSpec((None, 512, tm), lambda b, i: (b, 0, i)),
            pl.BlockSpec((None, A_HEADS * tm, HEAD_W), lambda b, i: (b, i, 0)),
            pl.BlockSpec((tm, 512), lambda b, i: (b * nt + i, 0)),
        ],
        out_shape=[
            jax.ShapeDtypeStruct((B, 512, T), F32),
            jax.ShapeDtypeStruct((B, 512, T), BF16),
            jax.ShapeDtypeStruct((B, A_HEADS * T, HEAD_W), F32),
            jax.ShapeDtypeStruct((M, 512), BF16),
        ],
        compiler_params=_cparams(("parallel", "parallel")),
        name="kvprep",
    )(z, z, gk_col)


def _rel_bucket_np(n):
    n = np.asarray(n, np.int32)
    max_exact = REL_BUCKETS // 2
    nf = np.maximum(n, max_exact).astype(np.float32)
    large = max_exact + (np.log(nf / np.float32(max_exact)) / np.float32(math.log(REL_MAX_DIST / max_exact))
                         * np.float32(REL_BUCKETS - max_exact)).astype(np.int32)
    large = np.minimum(large, REL_BUCKETS - 1)
    return np.where(n < max_exact, n, large).astype(np.int32)


def _bias_body(tab_ref, bkt_ref, o_ref):
    h = pl.program_id(0)
    bkt = bkt_ref[...]
    far = tab_ref[REL_BUCKETS - 1, h]
    acc = jnp.full(bkt.shape, NEG, F32)
    for b in range(REL_BUCKETS):
        acc = jnp.where(bkt == b, tab_ref[b, h] - far, acc)
    o_ref[...] = acc


def _bias_tiles(rel_table, buckets):
    R, C = buckets.shape
    return pl.pallas_call(
        _bias_body,
        grid=(A_HEADS,),
        in_specs=[pl.BlockSpec(memory_space=pltpu.SMEM), _resident((R, C))],
        out_specs=pl.BlockSpec((None, R, C), lambda h: (h, 0, 0)),
        out_shape=jax.ShapeDtypeStruct((A_HEADS, R, C), F32),
        compiler_params=_cparams(("arbitrary",)),
        name="bias_tiles",
    )(rel_table, jnp.asarray(buckets))


def _attn_body(lam_ref, q_ref, kT_ref, v_ref, bias_ref, gq_ref, gs_ref, o_ref,
               m_scr, l_scr, acc_scr, *, tile, out_scale):
    qi = pl.program_id(2)
    q = q_ref[...]
    lane = lax.broadcasted_iota(jnp.int32, q.shape, 1)
    lo = lane < A_QK_DIM
    q2 = q * q
    ms0 = jnp.sum(jnp.where(lo, q2, 0.0), axis=-1, keepdims=True) * (1.0 / A_QK_DIM)
    ms1 = jnp.sum(jnp.where(lo, 0.0, q2), axis=-1, keepdims=True) * (1.0 / A_QK_DIM)
    rs = jnp.where(lo, lax.rsqrt(ms0 + EPS), lax.rsqrt(ms1 + EPS))
    qn = q * rs * gq_ref[...] * (A_QK_DIM ** -0.5)
    qs = (jnp.where(lo, qn, 0.0).astype(BF16), jnp.where(lo, 0.0, qn).astype(BF16))

    m_scr[...] = jnp.full(m_scr.shape, -jnp.inf, F32)
    l_scr[...] = jnp.zeros(l_scr.shape, F32)
    acc_scr[...] = jnp.zeros(acc_scr.shape, F32)

    def step(ks, bias):
        kt = kT_ref[:, pl.ds(ks, tile)]
        vt = v_ref[pl.ds(ks, tile), :]
        for mp in range(2):
            s = jnp.dot(qs[mp], kt, preferred_element_type=F32)
            if bias is not None:
                s = s + bias
            m_prev = m_scr[mp]
            m_new = jnp.maximum(m_prev, jnp.max(s, axis=-1, keepdims=True))
            alpha = jnp.exp(m_prev - m_new)
            p = jnp.exp(s - m_new)
            l_scr[mp] = alpha * l_scr[mp] + jnp.sum(p, axis=-1, keepdims=True)
            acc_scr[mp] = alpha * acc_scr[mp] + jnp.dot(p.astype(BF16), vt, preferred_element_type=F32)
            m_scr[mp] = m_new

    def far_step(i, carry):
        step(pl.multiple_of(i * tile, tile), None)
        return carry

    lax.fori_loop(0, jnp.maximum(qi - 1, 0), far_step, 0)

    @pl.when(qi >= 1)
    def _():
        step(pl.multiple_of((qi - 1) * tile, tile), bias_ref[:, 0:tile])

    step(pl.multiple_of(qi * tile, tile), bias_ref[:, tile:2 * tile])

    o = acc_scr[0] / l_scr[0] - lam_ref[0] * (acc_scr[1] / l_scr[1])
    ms = jnp.mean(o * o, axis=-1, keepdims=True)
    o_ref[...] = (o * lax.rsqrt(ms + EPS) * gs_ref[...] * out_scale).astype(o_ref.dtype)


def _attn_prompt(lam, z, kTb, vb, bias, gq2, gs, B, T, out_scale, tile=ATT_TILE):
    nq = T // tile
    M = B * T
    return pl.pallas_call(
        functools.partial(_attn_body, tile=tile, out_scale=out_scale),
        grid=(B, A_HEADS, nq),
        in_specs=[
            pl.BlockSpec(memory_space=pltpu.SMEM),
            pl.BlockSpec((tile, HEAD_W), lambda b, h, i: (b * nq + i, ZC_Q * 4 + h)),
            pl.BlockSpec((None, HEAD_W, T), lambda b, h, i: (b, h, 0)),
            pl.BlockSpec((T, HEAD_W), lambda b, h, i: (b, h)),
            pl.BlockSpec((None, tile, 2 * tile), lambda b, h, i: (h, 0, 0)),
            _resident((1, HEAD_W)),
            _resident((1, HEAD_W)),
        ],
        out_specs=pl.BlockSpec((tile, HEAD_W), lambda b, h, i: (b * nq + i, h)),
        out_shape=jax.ShapeDtypeStruct((M, A_HEADS * HEAD_W), BF16),
        scratch_shapes=[
            pltpu.VMEM((2, tile, 1), F32),
            pltpu.VMEM((2, tile, 1), F32),
            pltpu.VMEM((2, tile, HEAD_W), F32),
        ],
        compiler_params=_cparams(("parallel", "parallel", "arbitrary")),
        name="attn_prompt",
    )(lam, z, kTb, vb, bias, gq2, gs)


def _pool_mix(ext_ref, base, n, pos, w_ref, sc_ref, o_ref):
    for g, win in enumerate(P_WINDOWS):
        sl = slice(g * HEAD_W, (g + 1) * HEAD_W)
        ug = ext_ref[base:base + n, sl]
        s = ug
        for j in range(1, win):
            s = s + ext_ref[base - j:base - j + n, sl]
        cnt = jnp.minimum(pos + 1, win).astype(F32)
        pooled = s / cnt - ug
        mixed = jnp.dot(pooled.astype(BF16), w_ref[g], preferred_element_type=F32) * sc_ref[:, sl]
        o_ref[:, sl] = mixed.astype(o_ref.dtype)


def _pool_body(u_ref, w_ref, sc_ref, o_ref, ext_scr, *, tm):
    i = pl.program_id(1)
    halo = 16

    @pl.when(i == 0)
    def _():
        ext_scr[0:halo, :] = jnp.zeros((halo, ext_scr.shape[1]), F32)

    @pl.when(i > 0)
    def _():
        ext_scr[0:halo, :] = ext_scr[tm:tm + halo, :]

    ext_scr[halo:halo + tm, :] = u_ref[...]
    pos = i * tm + lax.broadcasted_iota(jnp.int32, (tm, 1), 0)
    _pool_mix(ext_scr, halo, tm, pos, w_ref, sc_ref, o_ref)


def _pool_prompt(z, w_pool, sc, B, T, tm=512):
    nt = T // tm
    return pl.pallas_call(
        functools.partial(_pool_body, tm=tm),
        grid=(B, nt),
        in_specs=[
            pl.BlockSpec((tm, 512), lambda b, i: (b * nt + i, ZC_U)),
            _resident((4, HEAD_W, HEAD_W)),
            _resident((1, 512)),
        ],
        out_specs=pl.BlockSpec((tm, 512), lambda b, i: (b * nt + i, 0)),
        out_shape=jax.ShapeDtypeStruct((B * T, 512), BF16),
        scratch_shapes=[pltpu.VMEM((tm + 16, 512), F32)],
        compiler_params=_cparams(("parallel", "arbitrary")),
        name="pool_prompt",
    )(z, w_pool, sc)


def _hgrn_gates(f, log_lb, log1m_lb, one_m_lb):
    ls = jnp.minimum(f, 0.0) - jnp.log(1.0 + jnp.exp(-jnp.abs(f)))
    c = log1m_lb + ls
    mx = jnp.maximum(log_lb, c)
    lg = mx + jnp.log(1.0 + jnp.exp(-jnp.abs(log_lb - c)))
    kk = one_m_lb * _sigmoid(-f)
    return lg, kk


def _hgrn_chunk(zq, f, v, zg, lbp, gh, ST, C):
    nb = C // HGRN_SUB
    q = _silu(zq)
    lg, kk = _hgrn_gates(f, lbp[0:1, :], lbp[1:2, :], lbp[2:3, :])
    r = lax.broadcasted_iota(jnp.int32, (C, C), 0)
    c = lax.broadcasted_iota(jnp.int32, (C, C), 1)
    tri = (r >= c).astype(F32)
    b = jnp.dot(tri, lg, preferred_element_type=F32, precision=lax.Precision.HIGHEST)

    rin = lax.broadcasted_iota(jnp.int32, (C, 1), 0) & (HGRN_SUB - 1)
    o = jnp.sum(q * kk, axis=-1, keepdims=True) * v
    for d in range(1, HGRN_SUB):
        valid = rin >= d
        e = jnp.exp(jnp.where(valid, b - pltpu.roll(b, d, 0), 0.0))
        a = jnp.sum(q * pltpu.roll(kk, d, 0) * e, axis=-1, keepdims=True)
        o = o + jnp.where(valid, a, 0.0) * pltpu.roll(v, d, 0)

    b3 = b.reshape(nb, HGRN_SUB, HEAD_W)
    bend = b3[:, HGRN_SUB - 1:HGRN_SUB, :]
    bstart = jnp.concatenate([jnp.zeros((1, 1, HEAD_W), F32), bend[:-1]], axis=0)
    qt = (q.reshape(nb, HGRN_SUB, HEAD_W) * jnp.exp(b3 - bstart)).reshape(C, HEAD_W).astype(BF16)
    kk3 = kk.reshape(nb, HGRN_SUB, HEAD_W)
    bdiff = (r >> 3) - (c >> 3)
    a_off = jnp.zeros((C, C), F32)
    for d in range(1, nb):
        bsh = jnp.concatenate([bend[d - 1:]] + [bend[nb - 1:nb]] * (d - 1), axis=0)
        kd = (kk3 * jnp.exp(bsh - b3)).reshape(C, HEAD_W).astype(BF16)
        panel = lax.dot_general(qt, kd, (((1,), (1,)), ((), ())), preferred_element_type=F32)
        a_off = a_off + jnp.where(bdiff == d, panel, 0.0)
    vb = v.astype(BF16)
    o = o + jnp.dot(a_off.astype(BF16), vb, preferred_element_type=F32)

    qe = (q * jnp.exp(b)).astype(BF16)
    o = o + lax.dot_general(qe, ST.astype(BF16), (((1,), (1,)), ((), ())), preferred_element_type=F32)
    b_end = b[C - 1:C, :]
    ke = (kk * jnp.exp(b_end - b)).astype(BF16)
    ST_new = ST * jnp.exp(b_end) + lax.dot_general(vb, ke, (((0,), (0,)), ((), ())),
                                                    preferred_element_type=F32)
    ms = jnp.mean(o * o, axis=-1, keepdims=True)
    out = o * lax.rsqrt(ms + EPS) * gh * _silu(zg)
    return out, ST_new


def _hgrn_body(q_ref, f_ref, i_ref, g_ref, lbp_ref, gh_ref, o_ref, st_ref, ST_scr, *, tb, C):
    it = pl.program_id(2)

    @pl.when(it == 0)
    def _():
        ST_scr[...] = jnp.zeros(ST_scr.shape, F32)

    lbp = lbp_ref[...]
    gh = gh_ref[...]

    def chunk(ci, carry):
        rows = pl.ds(pl.multiple_of(ci * C, C), C)
        out, ST_new = _hgrn_chunk(q_ref[rows, :], f_ref[rows, :], i_ref[rows, :], g_ref[rows, :],
                                  lbp, gh, ST_scr[...], C)
        o_ref[rows, :] = out.astype(o_ref.dtype)
        ST_scr[...] = ST_new
        return carry

    lax.fori_loop(0, tb // C, chunk, 0)

    @pl.when(it == pl.num_programs(2) - 1)
    def _():
        st_ref[...] = ST_scr[...].T


def _hgrn_prompt(z, lbp, gh, B, T, tb=512, C=HGRN_CHUNK):
    nt = T // tb

    def zspec(cb):
        return pl.BlockSpec((tb, HEAD_W), lambda b, h, i: (b * nt + i, cb * 4 + h))

    return pl.pallas_call(
        functools.partial(_hgrn_body, tb=tb, C=C),
        grid=(B, A_HEADS, nt),
        in_specs=[zspec(ZC_HQ), zspec(ZC_HF), zspec(ZC_HI), zspec(ZC_HG),
                  pl.BlockSpec((3, HEAD_W), lambda b, h, i: (0, h)),
                  _resident((1, HEAD_W))],
        out_specs=[
            pl.BlockSpec((tb, HEAD_W), lambda b, h, i: (b * nt + i, h)),
            pl.BlockSpec((None, None, HEAD_W, HEAD_W), lambda b, h, i: (b, h, 0, 0)),
        ],
        out_shape=[
            jax.ShapeDtypeStruct((B * T, 512), BF16),
            jax.ShapeDtypeStruct((B, A_HEADS, HEAD_W, HEAD_W), F32),
        ],
        scratch_shapes=[pltpu.VMEM((HEAD_W, HEAD_W), F32)],
        compiler_params=_cparams(("parallel", "parallel", "arbitrary")),
        name="hgrn_prompt",
    )(z, z, z, z, lbp, gh)


def _merge_body(x_ref, a_ref, p_ref, o_ref, g0_ref, g1_ref, g2_ref, wb_ref, wo_ref, y_ref):
    merged = None
    for n, (br, gz) in enumerate(((a_ref, g0_ref), (p_ref, g1_ref), (o_ref, g2_ref))):
        proj = jnp.dot(br[...].astype(BF16), wb_ref[n], preferred_element_type=F32)
        term = _sigmoid(gz[...]) * proj
        merged = term if merged is None else merged + term
    y_ref[...] = x_ref[...] + jnp.dot(merged.astype(BF16), wo_ref[...], preferred_element_type=F32)


def _merge(x, a, p, o, z, wb, wo, tm):
    M, D = x.shape
    gcb = ZC_GZ // D

    def gspec(n):
        return pl.BlockSpec((tm, D), lambda i: (i, gcb + n))

    bspec = pl.BlockSpec((tm, BRANCH_W), lambda i: (i, 0))
    return pl.pallas_call(
        _merge_body,
        grid=(M // tm,),
        in_specs=[pl.BlockSpec((tm, D), lambda i: (i, 0)), bspec, bspec, bspec,
                  gspec(0), gspec(1), gspec(2),
                  _resident((N_BRANCH, BRANCH_W, D)), _resident((D, D))],
        out_specs=pl.BlockSpec((tm, D), lambda i: (i, 0)),
        out_shape=jax.ShapeDtypeStruct((M, D), F32),
        compiler_params=_cparams(("parallel",)),
        name="merge",
    )(x, a, p, o, z, z, z, wb, wo)


def _mlp_body(x_ref, g_ref, wu_ref, wd_ref, y_ref, *, fc):
    x = x_ref[...]
    ms = jnp.mean(x * x, axis=-1, keepdims=True)
    hm = (x * lax.rsqrt(ms + EPS) * g_ref[...]).astype(BF16)
    acc = x
    for c in range(wu_ref.shape[1] // fc):
        u = jnp.dot(hm, wu_ref[:, c * fc:(c + 1) * fc], preferred_element_type=F32)
        r = jnp.maximum(u, 0.0)
        acc = acc + jnp.dot((r * r).astype(BF16), wd_ref[c * fc:(c + 1) * fc, :], preferred_element_type=F32)
    y_ref[...] = acc


def _mlp(x, g, wu, wd, tm, fc=1024):
    M, D = x.shape
    Fd = wu.shape[1]
    return pl.pallas_call(
        functools.partial(_mlp_body, fc=fc),
        grid=(M // tm,),
        in_specs=[pl.BlockSpec((tm, D), lambda i: (i, 0)), _resident((1, D)),
                  pl.BlockSpec((D, Fd), lambda i: (0, 0), pipeline_mode=pl.Buffered(1)),
                  pl.BlockSpec((Fd, D), lambda i: (0, 0), pipeline_mode=pl.Buffered(1))],
        out_specs=pl.BlockSpec((tm, D), lambda i: (i, 0)),
        out_shape=jax.ShapeDtypeStruct((M, D), F32),
        compiler_params=_cparams(("parallel",)),
        name="mlp",
    )(x, g, wu, wd)


def _sprep_body(q_ref, k_ref, gq_ref, gk_ref, qn_ref, kT_ref, ss_ref):
    qnT = _group_norm_T(q_ref[...].T, gq_ref[...]) * (A_QK_DIM ** -0.5)
    knT = _group_norm_T(k_ref[...].T, gk_ref[...])
    qn_ref[...] = qnT.T
    kT_ref[...] = knT
    ss_ref[...] = jnp.sum((qnT * knT).reshape(8, A_QK_DIM, qnT.shape[1]), axis=1)


def _sprep(zs, gq_col, gk_col):
    Bs = zs.shape[0]
    return pl.pallas_call(
        _sprep_body,
        grid=(1,),
        in_specs=[pl.BlockSpec((Bs, 512), lambda i: (0, ZC_Q)),
                  pl.BlockSpec((Bs, 512), lambda i: (0, ZC_K)),
                  _resident((512, 1)), _resident((512, 1))],
        out_specs=[_resident((Bs, 512)), _resident((512, Bs)), _resident((8, Bs))],
        out_shape=[jax.ShapeDtypeStruct((Bs, 512), F32),
                   jax.ShapeDtypeStruct((512, Bs), F32),
                   jax.ShapeDtypeStruct((8, Bs), F32)],
        compiler_params=_cparams(("arbitrary",)),
        name="sample_prep",
    )(zs, zs, gq_col, gk_col)


def _decode_body(pt_ref, lam_ref, qn_ref, ss_ref, v_ref, bias_ref, gs_ref, *rest, npages, out_scale):
    k_refs = rest[:npages]
    v_refs = rest[npages:2 * npages]
    o_ref = rest[2 * npages]
    b = pl.program_id(0)
    Bs = qn_ref.shape[0]

    q_row = qn_ref[pl.ds(b, 1), :]
    row8 = lax.broadcasted_iota(jnp.int32, (8, 512), 0)
    grp = lax.broadcasted_iota(jnp.int32, (8, 512), 1) >> 6
    wq = jnp.where(grp == row8, jnp.broadcast_to(q_row, (8, 512)), 0.0).astype(BF16)

    s_pages = [jnp.dot(wq, k_refs[p][...].astype(BF16), preferred_element_type=F32)
               for p in range(npages)]
    s_pages[-1] = s_pages[-1] + bias_ref[:, 0:PAGE]
    lane_b = lax.broadcasted_iota(jnp.int32, (8, Bs), 1)
    s_self = (jnp.sum(jnp.where(lane_b == b, ss_ref[...], 0.0), axis=-1, keepdims=True)
              + bias_ref[:, PAGE:PAGE + 1])

    m = s_self
    for s in s_pages:
        m = jnp.maximum(m, jnp.max(s, axis=-1, keepdims=True))
    p_self = jnp.exp(s_self - m)
    p_pages = [jnp.exp(s - m) for s in s_pages]
    l = p_self
    for p in p_pages:
        l = l + jnp.sum(p, axis=-1, keepdims=True)
    row1 = lax.broadcasted_iota(jnp.int32, (8, 1), 0)
    coef = jnp.where((row1 & 1) == 0, 1.0, -lam_ref[0]) / l

    def pair(x):
        return x + pltpu.roll(x, 7, 0)

    w_self = pair(jnp.broadcast_to(p_self * coef, (8, HEAD_W)))
    accs = [jnp.zeros((8, HEAD_W), F32) for _ in range(A_HEADS)]
    for pg in range(npages):
        wp = pair(p_pages[pg] * coef).astype(BF16)
        for h in range(A_HEADS):
            vh = v_refs[pg][pl.ds(h, PAGE, stride=A_HEADS), :].astype(BF16)
            accs[h] = accs[h] + jnp.dot(wp, vh, preferred_element_type=F32)
    v_new = v_ref[pl.ds(b, 1), :]
    outs = []
    for h in range(A_HEADS):
        oh = accs[h][2 * h:2 * h + 1, :] + w_self[2 * h:2 * h + 1, :] * v_new[:, h * HEAD_W:(h + 1) * HEAD_W]
        ms = jnp.mean(oh * oh, axis=-1, keepdims=True)
        outs.append(oh * lax.rsqrt(ms + EPS) * gs_ref[...] * out_scale)
    o_ref[pl.ds(b, 1), :] = jnp.concatenate(outs, axis=1)


def _decode(pt_flat, lam, qn, ss, zs, bias_dec, gs, ckT, cv, layer, npages, out_scale):
    Bs = qn.shape[0]

    def page_spec(p):
        return pl.BlockSpec((None, None, 512, PAGE),
                            lambda b, pt, p=p: (layer, pt[b * npages + p], 0, 0))

    in_specs = [
        pl.BlockSpec(memory_space=pltpu.SMEM),
        pl.BlockSpec((Bs, 512), lambda b, pt: (0, 0)),
        pl.BlockSpec((8, Bs), lambda b, pt: (0, 0)),
        pl.BlockSpec((Bs, 512), lambda b, pt: (0, ZC_V)),
        pl.BlockSpec((8, 2 * PAGE), lambda b, pt: (0, 0)),
        pl.BlockSpec((1, HEAD_W), lambda b, pt: (0, 0)),
    ] + [page_spec(p) for p in range(npages)] * 2
    return pl.pallas_call(
        functools.partial(_decode_body, npages=npages, out_scale=out_scale),
        grid_spec=pltpu.PrefetchScalarGridSpec(
            num_scalar_prefetch=1,
            grid=(Bs,),
            in_specs=in_specs,
            out_specs=pl.BlockSpec((Bs, 512), lambda b, pt: (0, 0)),
        ),
        out_shape=jax.ShapeDtypeStruct((Bs, 512), F32),
        compiler_params=_cparams(("arbitrary",)),
        name="decode_attn",
    )(pt_flat, lam, qn, ss, zs, bias_dec, gs, *([ckT] * npages), *([cv] * npages))


def _smix_body(q_ref, f_ref, i_ref, g_ref, u_ref, sp_ref, s0_ref, lbp_ref, gh_ref, wp_ref, sc_ref,
               o_ref, p_ref, spn_ref, s1_ref, o_scr, *, ns, past_len):
    q = _silu(q_ref[...])
    lg, kk = _hgrn_gates(f_ref[...], lbp_ref[0:1, :], lbp_ref[1:2, :], lbp_ref[2:3, :])
    g = jnp.exp(lg)
    v = i_ref[...]
    eye = (lax.broadcasted_iota(jnp.int32, (HEAD_W, HEAD_W), 0)
           == lax.broadcasted_iota(jnp.int32, (HEAD_W, HEAD_W), 1))

    def col_of(row):
        return jnp.sum(jnp.where(eye, jnp.broadcast_to(row, eye.shape), 0.0), axis=-1, keepdims=True)

    for s in range(ns):
        for h in range(A_HEADS):
            sl = slice(h * HEAD_W, (h + 1) * HEAD_W)
            S_new = (col_of(g[s:s + 1, sl]) * s0_ref[s, h]
                     + col_of(kk[s:s + 1, sl]) * v[s:s + 1, sl])
            s1_ref[s, h] = S_new
            o_scr[s:s + 1, sl] = jnp.sum(col_of(q[s:s + 1, sl]) * S_new, axis=0, keepdims=True)
    zg = g_ref[...]
    for h in range(A_HEADS):
        sl = slice(h * HEAD_W, (h + 1) * HEAD_W)
        oh = o_scr[:, sl]
        ms = jnp.mean(oh * oh, axis=-1, keepdims=True)
        o_ref[:, sl] = oh * lax.rsqrt(ms + EPS) * gh_ref[...] * _silu(zg[:, sl])

    u = u_ref[...]
    for g_i, win in enumerate(P_WINDOWS):
        sl = slice(g_i * HEAD_W, (g_i + 1) * HEAD_W)
        ug = u[:, sl]
        sacc = ug
        for j in range(1, win):
            sacc = sacc + sp_ref[POOL_BUF - j][:, sl]
        cnt = float(min(past_len + 1, win))
        pooled = sacc / cnt - ug
        p_ref[:, sl] = (jnp.dot(pooled.astype(BF16), wp_ref[g_i], preferred_element_type=F32)
                        * sc_ref[:, sl])
    for j in range(POOL_BUF - 1):
        spn_ref[j] = sp_ref[j + 1]
    spn_ref[POOL_BUF - 1] = u


def _smix(zs, sp_l, s0_l, lbp, gh, w_pool, sc, past_len, ns=8):
    Bs = zs.shape[0]

    def zspec(cb):
        return pl.BlockSpec((ns, 512), lambda i: (i, cb))

    return pl.pallas_call(
        functools.partial(_smix_body, ns=ns, past_len=past_len),
        grid=(Bs // ns,),
        in_specs=[zspec(ZC_HQ), zspec(ZC_HF), zspec(ZC_HI), zspec(ZC_HG), zspec(ZC_U),
                  pl.BlockSpec((POOL_BUF, ns, 512), lambda i: (0, i, 0)),
                  pl.BlockSpec((ns, A_HEADS, HEAD_W, HEAD_W), lambda i: (i, 0, 0, 0)),
                  _resident((3, 512)), _resident((1, HEAD_W)),
                  _resident((4, HEAD_W, HEAD_W)), _resident((1, 512))],
        out_specs=[pl.BlockSpec((ns, 512), lambda i: (i, 0)),
                   pl.BlockSpec((ns, 512), lambda i: (i, 0)),
                   pl.BlockSpec((POOL_BUF, ns, 512), lambda i: (0, i, 0)),
                   pl.BlockSpec((ns, A_HEADS, HEAD_W, HEAD_W), lambda i: (i, 0, 0, 0))],
        out_shape=[jax.ShapeDtypeStruct((Bs, 512), F32),
                   jax.ShapeDtypeStruct((Bs, 512), F32),
                   jax.ShapeDtypeStruct((POOL_BUF, Bs, 512), F32),
                   jax.ShapeDtypeStruct((Bs, A_HEADS, HEAD_W, HEAD_W), F32)],
        scratch_shapes=[pltpu.VMEM((ns, 512), F32)],
        compiler_params=_cparams(("parallel",)),
        name="sample_mix",
    )(zs, zs, zs, zs, zs, sp_l, s0_l, lbp, gh, w_pool, sc)


def kernel(x_prompt, x_sample, cache_k, cache_v, state_pool, state_hgrn, page_table, rel_table, lb_param, w_in, g_mix, g_q, g_k, lam_p, g_sub, w_pool, pool_scale, g_h, w_branch, w_out, g_mlp, w_up, w_down):
    B, T, D = x_prompt.shape
    Bs = x_sample.shape[0]
    depth, n_phys = cache_k.shape[:2]
    npages = page_table.shape[1]
    past_len = npages * PAGE
    tile = ATT_TILE

    lb_all = jnp.cumsum(jax.nn.softmax(lb_param.astype(F32), axis=0), axis=0)
    lb_all = lb_all - lb_all[:1]
    lbp_all = jnp.stack([jnp.log(lb_all), jnp.log1p(-lb_all), 1.0 - lb_all], axis=1)
    lp = lam_p.astype(F32)
    lam_dyn = jnp.exp(jnp.sum(lp[:, 0] * lp[:, 1], axis=-1)) - jnp.exp(jnp.sum(lp[:, 2] * lp[:, 3], axis=-1))
    w_in_b, w_pool_b, w_branch_b = w_in.astype(BF16), w_pool.astype(BF16), w_branch.astype(BF16)
    w_out_b, w_up_b, w_down_b = w_out.astype(BF16), w_up.astype(BF16), w_down.astype(BF16)

    rr = np.arange(tile)[:, None]
    cc = np.arange(2 * tile)[None, :]
    dist = rr + tile - cc
    bkt_prompt = np.where(dist >= 0, _rel_bucket_np(np.maximum(dist, 0)), -1).astype(np.int32)
    bias_prompt = _bias_tiles(rel_table, bkt_prompt)
    dd = np.concatenate([PAGE - np.arange(PAGE), np.zeros(PAGE, np.int64)])
    bkt_dec = np.broadcast_to(_rel_bucket_np(dd)[None, :], (8, 2 * PAGE)).astype(np.int32)
    bias_dec_h = _bias_tiles(rel_table, bkt_dec)
    bias_dec = jnp.repeat(bias_dec_h[:, 0, :], 2, axis=0)

    ckT = jnp.transpose(cache_k, (0, 1, 3, 4, 5, 2)).reshape(depth, n_phys, 512, PAGE)
    cv4 = cache_v.reshape(depth, n_phys, A_HEADS * PAGE, HEAD_W)
    sp_t = jnp.transpose(state_pool, (0, 2, 1, 3))
    pt_flat = page_table.reshape(-1).astype(jnp.int32)

    xp = x_prompt.reshape(B * T, D)
    xs = x_sample.reshape(Bs, D)
    kp_l, vp_l, ks_l, vs_l, pp_l, ps_l, sp_l, ss_l = [], [], [], [], [], [], [], []
    for l in range(depth):
        lam_init = 0.8 - 0.6 * math.exp(-0.3 * l)
        out_scale = 1.0 - lam_init
        lam = (lam_dyn[l] + lam_init).reshape(1).astype(F32)
        gmix = g_mix[l].reshape(1, D)
        gq_col = jnp.tile(g_q[l], 8).reshape(512, 1)
        gk_col = jnp.tile(g_k[l], 8).reshape(512, 1)
        gq2 = jnp.tile(g_q[l], 2).reshape(1, HEAD_W)
        gs = g_sub[l].reshape(1, HEAD_W)
        gh = g_h[l].reshape(1, HEAD_W)
        sc = pool_scale[l].reshape(1, 512)
        gmlp = g_mlp[l].reshape(1, D)
        lbp = lbp_all[l]

        z = _inproj(xp, gmix, w_in_b[l], tm=512)
        kT, kTb, v4, vb = _kvprep(z, gk_col, B, T)
        a = _attn_prompt(lam, z, kTb, vb, bias_prompt, gq2, gs, B, T, out_scale)
        p = _pool_prompt(z, w_pool_b[l], sc, B, T)
        o, st = _hgrn_prompt(z, lbp, gh, B, T)
        x1 = _merge(xp, a, p, o, z, w_branch_b[l], w_out_b[l], tm=512)
        xp = _mlp(x1, gmlp, w_up_b[l], w_down_b[l], tm=512)
        kp_l.append(kT)
        vp_l.append(v4)
        pp_l.append(z.reshape(B, T, -1)[:, T - POOL_BUF:, ZC_U * 512:(ZC_U + 1) * 512])
        sp_l.append(st)

        zs = _inproj(xs, gmix, w_in_b[l], tm=Bs)
        qn, kTs, ssf = _sprep(zs, gq_col, gk_col)
        a_s = _decode(pt_flat, lam, qn, ssf, zs, bias_dec, gs, ckT, cv4, l, npages, out_scale)
        o_s, p_s, spn, s1 = _smix(zs, sp_t[l], state_hgrn[l], lbp, gh, w_pool_b[l], sc, past_len)
        x1s = _merge(xs, a_s, p_s, o_s, zs, w_branch_b[l], w_out_b[l], tm=Bs)
        xs = _mlp(x1s, gmlp, w_up_b[l], w_down_b[l], tm=Bs)
        ks_l.append(kTs)
        vs_l.append(zs[:, ZC_V * 512:(ZC_V + 1) * 512])
        ps_l.append(spn)
        ss_l.append(s1)

    k_prompt = jnp.transpose(jnp.stack(kp_l).reshape(depth, B, A_HEADS, 2, A_QK_DIM, T), (0, 1, 5, 2, 3, 4))
    v_prompt = jnp.stack(vp_l).reshape(depth, B, T, A_HEADS, A_V_DIM)
    k_sample = jnp.transpose(jnp.stack(ks_l).reshape(depth, A_HEADS, 2, A_QK_DIM, Bs), (0, 4, 1, 2, 3))[:, :, None]
    v_sample = jnp.stack(vs_l).reshape(depth, Bs, 1, A_HEADS, A_V_DIM)
    pool_prompt = jnp.stack(pp_l)
    pool_sample = jnp.transpose(jnp.stack(ps_l), (0, 2, 1, 3))
    return (xp.reshape(B, T, D), xs.reshape(Bs, 1, D), k_prompt, v_prompt, k_sample, v_sample,
            pool_prompt, pool_sample, jnp.stack(sp_l), jnp.stack(ss_l))
```

```python
import functools
import math

import numpy as np
import jax
import jax.numpy as jnp
from jax import lax
from jax.experimental import pallas as pl
from jax.experimental.pallas import tpu as pltpu

F32 = jnp.float32
BF16 = jnp.bfloat16
EPS = 1e-6

A_HEADS = 4
A_QK_DIM = 64
A_V_DIM = 128
HEAD_W = 128
P_WINDOWS = (2, 4, 8, 16)
POOL_BUF = 15
N_BRANCH = 3
BRANCH_W = 512
REL_BUCKETS = 32
REL_MAX_DIST = 128
PAGE = 128
NEG = -1e30
LOG2E = math.log2(math.e)

ZC_Q, ZC_K, ZC_V, ZC_U, ZC_HQ, ZC_HF, ZC_HI, ZC_HG = range(8)
ZC_GZ = 8 * 512

VMEM_LIMIT = 48 * 1024 * 1024
ATT_TILE = 512
ATT_MAX_SPREAD = 100.0
HGRN_CHUNK = 64
HGRN_SUB = 8


def _cparams(sem):
    return pltpu.CompilerParams(dimension_semantics=sem, vmem_limit_bytes=VMEM_LIMIT)


def _resident(shape):
    nd = len(shape)
    return pl.BlockSpec(shape, lambda *_: (0,) * nd)


def _sigmoid(x):
    return 1.0 / (1.0 + jnp.exp(-x))


def _silu(x):
    return x * _sigmoid(x)


def _inproj_body(x_ref, g_ref, w_ref, z_ref):
    x = x_ref[...]
    ms = jnp.mean(x * x, axis=-1, keepdims=True)
    h = (x * lax.rsqrt(ms + EPS) * g_ref[...]).astype(BF16)
    z_ref[...] = jnp.dot(h, w_ref[...], preferred_element_type=F32)


def _inproj(x, g, w, tm):
    M, D = x.shape
    N = w.shape[1]
    tn = N // 2
    return pl.pallas_call(
        _inproj_body,
        grid=(N // tn, M // tm),
        in_specs=[
            pl.BlockSpec((tm, D), lambda j, i: (i, 0)),
            pl.BlockSpec((1, D), lambda j, i: (0, 0)),
            pl.BlockSpec((D, tn), lambda j, i: (0, j), pipeline_mode=pl.Buffered(1)),
        ],
        out_specs=pl.BlockSpec((tm, tn), lambda j, i: (i, j)),
        out_shape=jax.ShapeDtypeStruct((M, N), F32),
        compiler_params=_cparams(("parallel", "parallel")),
        name="inproj",
    )(x, g, w)


def _group_norm_T(xT, gcol):
    n = xT.shape[1]
    x3 = xT.reshape(8, A_QK_DIM, n)
    ms = jnp.mean(x3 * x3, axis=1, keepdims=True)
    return (x3 * lax.rsqrt(ms + EPS)).reshape(8 * A_QK_DIM, n) * gcol


def _kvprep_body(k_ref, v_ref, gk_ref, kT_ref, kTb_ref, v4_ref, vb1_ref, *, tm):
    kn = _group_norm_T(k_ref[...].T, gk_ref[...])
    kT_ref[...] = kn
    kTb_ref[...] = kn.astype(BF16)
    v = v_ref[...]
    ones = jnp.ones((tm, HEAD_W), BF16)
    for h in range(A_HEADS):
        vh = v[:, h * HEAD_W:(h + 1) * HEAD_W]
        v4_ref[pl.ds(h, tm, stride=A_HEADS), :] = vh
        vb1_ref[:, 2 * h * HEAD_W:(2 * h + 1) * HEAD_W] = vh.astype(BF16)
        vb1_ref[:, (2 * h + 1) * HEAD_W:(2 * h + 2) * HEAD_W] = ones


def _kvprep(z, gk_col, B, T, tm=512):
    nt = T // tm
    M = B * T
    return pl.pallas_call(
        functools.partial(_kvprep_body, tm=tm),
        grid=(B, nt),
        in_specs=[
            pl.BlockSpec((tm, 512), lambda b, i: (b * nt + i, ZC_K)),
            pl.BlockSpec((tm, 512), lambda b, i: (b * nt + i, ZC_V)),
            pl.BlockSpec((512, 1), lambda b, i: (0, 0)),
        ],
        out_specs=[
            pl.BlockSpec((None, 512, tm), lambda b, i: (b, 0, i)),
            pl.BlockSpec((None, 512, tm), lambda b, i: (b, 0, i)),
            pl.BlockSpec((None, A_HEADS * tm, HEAD_W), lambda b, i: (b, i, 0)),
            pl.BlockSpec((tm, 2 * 512), lambda b, i: (b * nt + i, 0)),
        ],
        out_shape=[
            jax.ShapeDtypeStruct((B, 512, T), F32),
            jax.ShapeDtypeStruct((B, 512, T), BF16),
            jax.ShapeDtypeStruct((B, A_HEADS * T, HEAD_W), F32),
            jax.ShapeDtypeStruct((M, 2 * 512), BF16),
        ],
        compiler_params=_cparams(("parallel", "parallel")),
        name="kvprep",
    )(z, z, gk_col)


def _rel_bucket_np(n):
    n = np.asarray(n, np.int32)
    max_exact = REL_BUCKETS // 2
    nf = np.maximum(n, max_exact).astype(np.float32)
    large = max_exact + (np.log(nf / np.float32(max_exact)) / np.float32(math.log(REL_MAX_DIST / max_exact))
                         * np.float32(REL_BUCKETS - max_exact)).astype(np.int32)
    large = np.minimum(large, REL_BUCKETS - 1)
    return np.where(n < max_exact, n, large).astype(np.int32)


def _bias_body(tab_ref, bkt_ref, o_ref, *, scale):
    h = pl.program_id(0)
    bkt = bkt_ref[...]
    far = tab_ref[REL_BUCKETS - 1, h]
    acc = jnp.full(bkt.shape, NEG, F32)
    for b in range(REL_BUCKETS):
        acc = jnp.where(bkt == b, (tab_ref[b, h] - far) * scale, acc)
    o_ref[...] = acc


def _bias_tiles(rel_table, buckets, scale):
    R, C = buckets.shape
    return pl.pallas_call(
        functools.partial(_bias_body, scale=scale),
        grid=(A_HEADS,),
        in_specs=[pl.BlockSpec(memory_space=pltpu.SMEM), _resident((R, C))],
        out_specs=pl.BlockSpec((None, R, C), lambda h: (h, 0, 0)),
        out_shape=jax.ShapeDtypeStruct((A_HEADS, R, C), F32),
        compiler_params=_cparams(("arbitrary",)),
        name="bias_tiles",
    )(rel_table, jnp.asarray(buckets))


def _attn_body(par_ref, q_ref, kT_ref, v1_ref, bias_ref, gq_ref, gs_ref, o_ref,
               mb_scr, m_scr, acc_scr, *, tile, out_scale):
    h = pl.program_id(1)
    qi = pl.program_id(2)
    q = q_ref[...]
    lane = lax.broadcasted_iota(jnp.int32, q.shape, 1)
    lo = lane < A_QK_DIM
    q2 = q * q
    ms0 = jnp.sum(jnp.where(lo, q2, 0.0), axis=-1, keepdims=True) * (1.0 / A_QK_DIM)
    ms1 = jnp.sum(jnp.where(lo, 0.0, q2), axis=-1, keepdims=True) * (1.0 / A_QK_DIM)
    rs = jnp.where(lo, lax.rsqrt(ms0 + EPS), lax.rsqrt(ms1 + EPS))
    qn = q * rs * gq_ref[...] * (A_QK_DIM ** -0.5 * LOG2E)
    qf = (jnp.where(lo, qn, 0.0), jnp.where(lo, 0.0, qn))
    qs = (qf[0].astype(BF16), qf[1].astype(BF16))
    acc_scr[...] = jnp.zeros(acc_scr.shape, F32)
    diag = pl.multiple_of(qi * tile, tile)
    sub = pl.multiple_of(jnp.maximum(qi - 1, 0) * tile, tile)

    def run(step):
        def far_step(i, carry):
            step(pl.multiple_of(i * tile, tile), None)
            return carry

        lax.fori_loop(0, jnp.maximum(qi - 1, 0), far_step, 0)

        @pl.when(qi >= 1)
        def _():
            step(sub, bias_ref[:, 0:tile])

        step(diag, bias_ref[:, tile:2 * tile])

    @pl.when(par_ref[1] > 0.5)
    def _():
        for mp in range(2):
            nq = jnp.sqrt(jnp.sum(qf[mp] * qf[mp], axis=-1, keepdims=True))
            mb_scr[mp] = jnp.broadcast_to(nq * par_ref[2] + par_ref[3 + h], (tile, tile))

        def step(ks, bias):
            kt = kT_ref[:, pl.ds(ks, tile)]
            vt = v1_ref[pl.ds(ks, tile), :]
            for mp in range(2):
                s = jnp.dot(qs[mp], kt, preferred_element_type=F32)
                if bias is not None:
                    s = s + bias
                p = jnp.exp2(s - mb_scr[mp]).astype(BF16)
                acc_scr[mp] += jnp.dot(p, vt, preferred_element_type=F32)

        run(step)

    @pl.when(par_ref[1] <= 0.5)
    def _():
        m_scr[...] = jnp.full(m_scr.shape, -jnp.inf, F32)

        def step(ks, bias):
            kt = kT_ref[:, pl.ds(ks, tile)]
            vt = v1_ref[pl.ds(ks, tile), :]
            for mp in range(2):
                s = jnp.dot(qs[mp], kt, preferred_element_type=F32)
                if bias is not None:
                    s = s + bias
                m_prev = m_scr[mp]
                m_new = jnp.maximum(m_prev, jnp.max(s, axis=-1, keepdims=True))
                p = jnp.exp2(s - m_new).astype(BF16)
                acc_scr[mp] = (jnp.exp2(m_prev - m_new) * acc_scr[mp]
                               + jnp.dot(p, vt, preferred_element_type=F32))
                m_scr[mp] = m_new

        run(step)

    a0 = acc_scr[0]
    a1 = acc_scr[1]
    o = a0[:, :HEAD_W] / a0[:, HEAD_W:] - par_ref[0] * (a1[:, :HEAD_W] / a1[:, HEAD_W:])
    ms = jnp.mean(o * o, axis=-1, keepdims=True)
    o_ref[...] = (o * lax.rsqrt(ms + EPS) * gs_ref[...] * out_scale).astype(o_ref.dtype)


def _attn_prompt(par, z, kTb, vb1, bias, gq2, gs, B, T, out_scale, tile=ATT_TILE):
    nq = T // tile
    M = B * T
    return pl.pallas_call(
        functools.partial(_attn_body, tile=tile, out_scale=out_scale),
        grid=(B, A_HEADS, nq),
        in_specs=[
            pl.BlockSpec(memory_space=pltpu.SMEM),
            pl.BlockSpec((tile, HEAD_W), lambda b, h, i: (b * nq + i, ZC_Q * 4 + h)),
            pl.BlockSpec((None, HEAD_W, T), lambda b, h, i: (b, h, 0)),
            pl.BlockSpec((T, 2 * HEAD_W), lambda b, h, i: (b, h)),
            pl.BlockSpec((None, tile, 2 * tile), lambda b, h, i: (h, 0, 0)),
            _resident((1, HEAD_W)),
            _resident((1, HEAD_W)),
        ],
        out_specs=pl.BlockSpec((tile, HEAD_W), lambda b, h, i: (b * nq + i, h)),
        out_shape=jax.ShapeDtypeStruct((M, A_HEADS * HEAD_W), BF16),
        scratch_shapes=[
            pltpu.VMEM((2, tile, tile), F32),
            pltpu.VMEM((2, tile, 1), F32),
            pltpu.VMEM((2, tile, 2 * HEAD_W), F32),
        ],
        compiler_params=_cparams(("parallel", "parallel", "arbitrary")),
        name="attn_prompt",
    )(par, z, kTb, vb1, bias, gq2, gs)


def _pool_mix(ext_ref, base, n, pos, w_ref, sc_ref, o_ref):
    for g, win in enumerate(P_WINDOWS):
        sl = slice(g * HEAD_W, (g + 1) * HEAD_W)
        ug = ext_ref[base:base + n, sl]
        s = ug
        for j in range(1, win):
            s = s + ext_ref[base - j:base - j + n, sl]
        cnt = jnp.minimum(pos + 1, win).astype(F32)
        pooled = s / cnt - ug
        mixed = jnp.dot(pooled.astype(BF16), w_ref[g], preferred_element_type=F32) * sc_ref[:, sl]
        o_ref[:, sl] = mixed.astype(o_ref.dtype)


def _pool_body(u_ref, w_ref, sc_ref, o_ref, ext_scr, *, tm):
    i = pl.program_id(1)
    halo = 16

    @pl.when(i == 0)
    def _():
        ext_scr[0:halo, :] = jnp.zeros((halo, ext_scr.shape[1]), F32)

    @pl.when(i > 0)
    def _():
        ext_scr[0:halo, :] = ext_scr[tm:tm + halo, :]

    ext_scr[halo:halo + tm, :] = u_ref[...]
    pos = i * tm + lax.broadcasted_iota(jnp.int32, (tm, 1), 0)
    _pool_mix(ext_scr, halo, tm, pos, w_ref, sc_ref, o_ref)


def _pool_prompt(z, w_pool, sc, B, T, tm=512):
    nt = T // tm
    return pl.pallas_call(
        functools.partial(_pool_body, tm=tm),
        grid=(B, nt),
        in_specs=[
            pl.BlockSpec((tm, 512), lambda b, i: (b * nt + i, ZC_U)),
            _resident((4, HEAD_W, HEAD_W)),
            _resident((1, 512)),
        ],
        out_specs=pl.BlockSpec((tm, 512), lambda b, i: (b * nt + i, 0)),
        out_shape=jax.ShapeDtypeStruct((B * T, 512), BF16),
        scratch_shapes=[pltpu.VMEM((tm + 16, 512), F32)],
        compiler_params=_cparams(("parallel", "arbitrary")),
        name="pool_prompt",
    )(z, w_pool, sc)


def _hgrn_gates(f, log_lb, log1m_lb, one_m_lb):
    ls = jnp.minimum(f, 0.0) - jnp.log(1.0 + jnp.exp(-jnp.abs(f)))
    c = log1m_lb + ls
    mx = jnp.maximum(log_lb, c)
    lg = mx + jnp.log(1.0 + jnp.exp(-jnp.abs(log_lb - c)))
    kk = one_m_lb * _sigmoid(-f)
    return lg, kk


def _hgrn_consts(C):
    r = lax.broadcasted_iota(jnp.int32, (C, C), 0)
    c = lax.broadcasted_iota(jnp.int32, (C, C), 1)
    tri = (r >= c).astype(F32)
    rin = lax.broadcasted_iota(jnp.int32, (C, 1), 0) & (HGRN_SUB - 1)
    validf = [(rin >= d).astype(F32) for d in range(HGRN_SUB)]
    bdiff = (r >> 3) - (c >> 3)
    bandf = [(bdiff == d).astype(F32) for d in range(C // HGRN_SUB)]
    return tri, validf, bandf


def _hgrn_head(q, kk, v, b, ST, consts, C):
    _, validf, bandf = consts
    nb = C // HGRN_SUB
    o = jnp.sum(q * kk, axis=-1, keepdims=True) * v
    for d in range(1, HGRN_SUB):
        e = jnp.exp(jnp.minimum(b - pltpu.roll(b, d, 0), 0.0))
        a = jnp.sum(q * pltpu.roll(kk, d, 0) * e, axis=-1, keepdims=True)
        o = o + (a * validf[d]) * pltpu.roll(v, d, 0)

    b3 = b.reshape(nb, HGRN_SUB, HEAD_W)
    bend = b3[:, HGRN_SUB - 1:HGRN_SUB, :]
    bstart = jnp.concatenate([jnp.zeros((1, 1, HEAD_W), F32), bend[:-1]], axis=0)
    qt = (q.reshape(nb, HGRN_SUB, HEAD_W) * jnp.exp(b3 - bstart)).reshape(C, HEAD_W).astype(BF16)
    kk3 = kk.reshape(nb, HGRN_SUB, HEAD_W)
    a_off = None
    for d in range(1, nb):
        bsh = jnp.concatenate([bend[d - 1:]] + [bend[nb - 1:nb]] * (d - 1), axis=0)
        kd = (kk3 * jnp.exp(bsh - b3)).reshape(C, HEAD_W).astype(BF16)
        panel = lax.dot_general(qt, kd, (((1,), (1,)), ((), ())), preferred_element_type=F32)
        a_off = panel * bandf[d] if a_off is None else a_off + panel * bandf[d]
    vb = v.astype(BF16)
    o = o + jnp.dot(a_off.astype(BF16), vb, preferred_element_type=F32)

    qe = (q * jnp.exp(b)).astype(BF16)
    o = o + lax.dot_general(qe, ST.astype(BF16), (((1,), (1,)), ((), ())), preferred_element_type=F32)
    b_end = b[C - 1:C, :]
    ke = (kk * jnp.exp(b_end - b)).astype(BF16)
    ST_new = ST * jnp.exp(b_end) + lax.dot_general(vb, ke, (((0,), (0,)), ((), ())),
                                                    preferred_element_type=F32)
    return o, ST_new


def _hgrn_body(q_ref, f_ref, i_ref, g_ref, lbp_ref, gh_ref, o_ref, st_ref, ST_scr, *, tb, C):
    it = pl.program_id(1)

    @pl.when(it == 0)
    def _():
        ST_scr[...] = jnp.zeros(ST_scr.shape, F32)

    lbp = lbp_ref[...]
    gh = gh_ref[...]
    consts = _hgrn_consts(C)

    def chunk(ci, carry):
        rows = pl.ds(pl.multiple_of(ci * C, C), C)
        q = _silu(q_ref[rows, :])
        lg, kk = _hgrn_gates(f_ref[rows, :], lbp[0:1, :], lbp[1:2, :], lbp[2:3, :])
        b = jnp.dot(consts[0], lg, preferred_element_type=F32, precision=lax.Precision.HIGHEST)
        v = i_ref[rows, :]
        gate = _silu(g_ref[rows, :])
        for h in range(A_HEADS):
            sl = slice(h * HEAD_W, (h + 1) * HEAD_W)
            o, ST_new = _hgrn_head(q[:, sl], kk[:, sl], v[:, sl], b[:, sl], ST_scr[h], consts, C)
            ST_scr[h] = ST_new
            ms = jnp.mean(o * o, axis=-1, keepdims=True)
            o_ref[rows, sl] = (o * lax.rsqrt(ms + EPS) * gh * gate[:, sl]).astype(o_ref.dtype)
        return carry

    lax.fori_loop(0, tb // C, chunk, 0)

    @pl.when(it == pl.num_programs(1) - 1)
    def _():
        for h in range(A_HEADS):
            st_ref[h] = ST_scr[h].T


def _hgrn_prompt(z, lbp, gh, B, T, tb=512, C=HGRN_CHUNK):
    nt = T // tb

    def zspec(cb):
        return pl.BlockSpec((tb, 512), lambda b, i: (b * nt + i, cb))

    return pl.pallas_call(
        functools.partial(_hgrn_body, tb=tb, C=C),
        grid=(B, nt),
        in_specs=[zspec(ZC_HQ), zspec(ZC_HF), zspec(ZC_HI), zspec(ZC_HG),
                  _resident((3, 512)), _resident((1, HEAD_W))],
        out_specs=[
            pl.BlockSpec((tb, 512), lambda b, i: (b * nt + i, 0)),
            pl.BlockSpec((None, A_HEADS, HEAD_W, HEAD_W), lambda b, i: (b, 0, 0, 0)),
        ],
        out_shape=[
            jax.ShapeDtypeStruct((B * T, 512), BF16),
            jax.ShapeDtypeStruct((B, A_HEADS, HEAD_W, HEAD_W), F32),
        ],
        scratch_shapes=[pltpu.VMEM((A_HEADS, HEAD_W, HEAD_W), F32)],
        compiler_params=_cparams(("parallel", "arbitrary")),
        name="hgrn_prompt",
    )(z, z, z, z, lbp, gh)


def _merge_body(x_ref, a_ref, p_ref, o_ref, g0_ref, g1_ref, g2_ref, wb_ref, wo_ref, y_ref):
    merged = None
    for n, (br, gz) in enumerate(((a_ref, g0_ref), (p_ref, g1_ref), (o_ref, g2_ref))):
        proj = jnp.dot(br[...].astype(BF16), wb_ref[n], preferred_element_type=F32)
        term = _sigmoid(gz[...]) * proj
        merged = term if merged is None else merged + term
    y_ref[...] = x_ref[...] + jnp.dot(merged.astype(BF16), wo_ref[...], preferred_element_type=F32)


def _merge(x, a, p, o, z, wb, wo, tm):
    M, D = x.shape
    gcb = ZC_GZ // D

    def gspec(n):
        return pl.BlockSpec((tm, D), lambda i: (i, gcb + n))

    bspec = pl.BlockSpec((tm, BRANCH_W), lambda i: (i, 0))
    return pl.pallas_call(
        _merge_body,
        grid=(M // tm,),
        in_specs=[pl.BlockSpec((tm, D), lambda i: (i, 0)), bspec, bspec, bspec,
                  gspec(0), gspec(1), gspec(2),
                  _resident((N_BRANCH, BRANCH_W, D)), _resident((D, D))],
        out_specs=pl.BlockSpec((tm, D), lambda i: (i, 0)),
        out_shape=jax.ShapeDtypeStruct((M, D), F32),
        compiler_params=_cparams(("parallel",)),
        name="merge",
    )(x, a, p, o, z, z, z, wb, wo)


def _mlp_body(x_ref, g_ref, wu_ref, wd_ref, y_ref, *, fc):
    x = x_ref[...]
    ms = jnp.mean(x * x, axis=-1, keepdims=True)
    hm = (x * lax.rsqrt(ms + EPS) * g_ref[...]).astype(BF16)
    acc = x
    for c in range(wu_ref.shape[1] // fc):
        u = jnp.dot(hm, wu_ref[:, c * fc:(c + 1) * fc], preferred_element_type=F32)
        r = jnp.maximum(u, 0.0)
        acc = acc + jnp.dot((r * r).astype(BF16), wd_ref[c * fc:(c + 1) * fc, :], preferred_element_type=F32)
    y_ref[...] = acc


def _mlp(x, g, wu, wd, tm, fc=1024):
    M, D = x.shape
    Fd = wu.shape[1]
    return pl.pallas_call(
        functools.partial(_mlp_body, fc=fc),
        grid=(M // tm,),
        in_specs=[pl.BlockSpec((tm, D), lambda i: (i, 0)), _resident((1, D)),
                  pl.BlockSpec((D, Fd), lambda i: (0, 0), pipeline_mode=pl.Buffered(1)),
                  pl.BlockSpec((Fd, D), lambda i: (0, 0), pipeline_mode=pl.Buffered(1))],
        out_specs=pl.BlockSpec((tm, D), lambda i: (i, 0)),
        out_shape=jax.ShapeDtypeStruct((M, D), F32),
        compiler_params=_cparams(("parallel",)),
        name="mlp",
    )(x, g, wu, wd)


def _sprep_body(q_ref, k_ref, gq_ref, gk_ref, qn_ref, kT_ref, ss_ref):
    qnT = _group_norm_T(q_ref[...].T, gq_ref[...]) * (A_QK_DIM ** -0.5)
    knT = _group_norm_T(k_ref[...].T, gk_ref[...])
    qn_ref[...] = qnT.T
    kT_ref[...] = knT
    ss_ref[...] = jnp.sum((qnT * knT).reshape(8, A_QK_DIM, qnT.shape[1]), axis=1)


def _sprep(zs, gq_col, gk_col):
    Bs = zs.shape[0]
    return pl.pallas_call(
        _sprep_body,
        grid=(1,),
        in_specs=[pl.BlockSpec((Bs, 512), lambda i: (0, ZC_Q)),
                  pl.BlockSpec((Bs, 512), lambda i: (0, ZC_K)),
                  _resident((512, 1)), _resident((512, 1))],
        out_specs=[_resident((Bs, 512)), _resident((512, Bs)), _resident((8, Bs))],
        out_shape=[jax.ShapeDtypeStruct((Bs, 512), F32),
                   jax.ShapeDtypeStruct((512, Bs), F32),
                   jax.ShapeDtypeStruct((8, Bs), F32)],
        compiler_params=_cparams(("arbitrary",)),
        name="sample_prep",
    )(zs, zs, gq_col, gk_col)


def _decode_body(pt_ref, lam_ref, qn_ref, ss_ref, v_ref, bias_ref, gs_ref, *rest, npages, out_scale):
    k_refs = rest[:npages]
    v_refs = rest[npages:2 * npages]
    o_ref = rest[2 * npages]
    b = pl.program_id(0)
    Bs = qn_ref.shape[0]

    q_row = qn_ref[pl.ds(b, 1), :]
    row8 = lax.broadcasted_iota(jnp.int32, (8, 512), 0)
    grp = lax.broadcasted_iota(jnp.int32, (8, 512), 1) >> 6
    wq = jnp.where(grp == row8, jnp.broadcast_to(q_row, (8, 512)), 0.0).astype(BF16)

    s_pages = [jnp.dot(wq, k_refs[p][...].astype(BF16), preferred_element_type=F32)
               for p in range(npages)]
    s_pages[-1] = s_pages[-1] + bias_ref[:, 0:PAGE]
    lane_b = lax.broadcasted_iota(jnp.int32, (8, Bs), 1)
    s_self = (jnp.sum(jnp.where(lane_b == b, ss_ref[...], 0.0), axis=-1, keepdims=True)
              + bias_ref[:, PAGE:PAGE + 1])

    m = s_self
    for s in s_pages:
        m = jnp.maximum(m, jnp.max(s, axis=-1, keepdims=True))
    p_self = jnp.exp(s_self - m)
    p_pages = [jnp.exp(s - m) for s in s_pages]
    l = p_self
    for p in p_pages:
        l = l + jnp.sum(p, axis=-1, keepdims=True)
    row1 = lax.broadcasted_iota(jnp.int32, (8, 1), 0)
    coef = jnp.where((row1 & 1) == 0, 1.0, -lam_ref[0]) / l

    def pair(x):
        return x + pltpu.roll(x, 7, 0)

    w_self = pair(jnp.broadcast_to(p_self * coef, (8, HEAD_W)))
    accs = [jnp.zeros((8, HEAD_W), F32) for _ in range(A_HEADS)]
    for pg in range(npages):
        wp = pair(p_pages[pg] * coef).astype(BF16)
        for h in range(A_HEADS):
            vh = v_refs[pg][pl.ds(h, PAGE, stride=A_HEADS), :].astype(BF16)
            accs[h] = accs[h] + jnp.dot(wp, vh, preferred_element_type=F32)
    v_new = v_ref[pl.ds(b, 1), :]
    outs = []
    for h in range(A_HEADS):
        oh = accs[h][2 * h:2 * h + 1, :] + w_self[2 * h:2 * h + 1, :] * v_new[:, h * HEAD_W:(h + 1) * HEAD_W]
        ms = jnp.mean(oh * oh, axis=-1, keepdims=True)
        outs.append(oh * lax.rsqrt(ms + EPS) * gs_ref[...] * out_scale)
    o_ref[pl.ds(b, 1), :] = jnp.concatenate(outs, axis=1)


def _decode(pt_flat, lam, qn, ss, zs, bias_dec, gs, ckT, cv, layer, npages, out_scale):
    Bs = qn.shape[0]

    def page_spec(p):
        return pl.BlockSpec((None, None, 512, PAGE),
                            lambda b, pt, p=p: (layer, pt[b * npages + p], 0, 0))

    in_specs = [
        pl.BlockSpec(memory_space=pltpu.SMEM),
        pl.BlockSpec((Bs, 512), lambda b, pt: (0, 0)),
        pl.BlockSpec((8, Bs), lambda b, pt: (0, 0)),
        pl.BlockSpec((Bs, 512), lambda b, pt: (0, ZC_V)),
        pl.BlockSpec((8, 2 * PAGE), lambda b, pt: (0, 0)),
        pl.BlockSpec((1, HEAD_W), lambda b, pt: (0, 0)),
    ] + [page_spec(p) for p in range(npages)] * 2
    return pl.pallas_call(
        functools.partial(_decode_body, npages=npages, out_scale=out_scale),
        grid_spec=pltpu.PrefetchScalarGridSpec(
            num_scalar_prefetch=1,
            grid=(Bs,),
            in_specs=in_specs,
            out_specs=pl.BlockSpec((Bs, 512), lambda b, pt: (0, 0)),
        ),
        out_shape=jax.ShapeDtypeStruct((Bs, 512), F32),
        compiler_params=_cparams(("arbitrary",)),
        name="decode_attn",
    )(pt_flat, lam, qn, ss, zs, bias_dec, gs, *([ckT] * npages), *([cv] * npages))


def _smix_body(q_ref, f_ref, i_ref, g_ref, u_ref, sp_ref, s0_ref, lbp_ref, gh_ref, wp_ref, sc_ref,
               o_ref, p_ref, spn_ref, s1_ref, o_scr, *, ns, past_len):
    q = _silu(q_ref[...])
    lg, kk = _hgrn_gates(f_ref[...], lbp_ref[0:1, :], lbp_ref[1:2, :], lbp_ref[2:3, :])
    g = jnp.exp(lg)
    v = i_ref[...]
    eye = (lax.broadcasted_iota(jnp.int32, (HEAD_W, HEAD_W), 0)
           == lax.broadcasted_iota(jnp.int32, (HEAD_W, HEAD_W), 1))

    def col_of(row):
        return jnp.sum(jnp.where(eye, jnp.broadcast_to(row, eye.shape), 0.0), axis=-1, keepdims=True)

    for s in range(ns):
        for h in range(A_HEADS):
            sl = slice(h * HEAD_W, (h + 1) * HEAD_W)
            S_new = (col_of(g[s:s + 1, sl]) * s0_ref[s, h]
                     + col_of(kk[s:s + 1, sl]) * v[s:s + 1, sl])
            s1_ref[s, h] = S_new
            o_scr[s:s + 1, sl] = jnp.sum(col_of(q[s:s + 1, sl]) * S_new, axis=0, keepdims=True)
    zg = g_ref[...]
    for h in range(A_HEADS):
        sl = slice(h * HEAD_W, (h + 1) * HEAD_W)
        oh = o_scr[:, sl]
        ms = jnp.mean(oh * oh, axis=-1, keepdims=True)
        o_ref[:, sl] = oh * lax.rsqrt(ms + EPS) * gh_ref[...] * _silu(zg[:, sl])

    u = u_ref[...]
    for g_i, win in enumerate(P_WINDOWS):
        sl = slice(g_i * HEAD_W, (g_i + 1) * HEAD_W)
        ug = u[:, sl]
        sacc = ug
        for j in range(1, win):
            sacc = sacc + sp_ref[POOL_BUF - j][:, sl]
        cnt = float(min(past_len + 1, win))
        pooled = sacc / cnt - ug
        p_ref[:, sl] = (jnp.dot(pooled.astype(BF16), wp_ref[g_i], preferred_element_type=F32)
                        * sc_ref[:, sl])
    for j in range(POOL_BUF - 1):
        spn_ref[j] = sp_ref[j + 1]
    spn_ref[POOL_BUF - 1] = u


def _smix(zs, sp_l, s0_l, lbp, gh, w_pool, sc, past_len, ns=8):
    Bs = zs.shape[0]

    def zspec(cb):
        return pl.BlockSpec((ns, 512), lambda i: (i, cb))

    return pl.pallas_call(
        functools.partial(_smix_body, ns=ns, past_len=past_len),
        grid=(Bs // ns,),
        in_specs=[zspec(ZC_HQ), zspec(ZC_HF), zspec(ZC_HI), zspec(ZC_HG), zspec(ZC_U),
                  pl.BlockSpec((POOL_BUF, ns, 512), lambda i: (0, i, 0)),
                  pl.BlockSpec((ns, A_HEADS, HEAD_W, HEAD_W), lambda i: (i, 0, 0, 0)),
                  _resident((3, 512)), _resident((1, HEAD_W)),
                  _resident((4, HEAD_W, HEAD_W)), _resident((1, 512))],
        out_specs=[pl.BlockSpec((ns, 512), lambda i: (i, 0)),
                   pl.BlockSpec((ns, 512), lambda i: (i, 0)),
                   pl.BlockSpec((POOL_BUF, ns, 512), lambda i: (0, i, 0)),
                   pl.BlockSpec((ns, A_HEADS, HEAD_W, HEAD_W), lambda i: (i, 0, 0, 0))],
        out_shape=[jax.ShapeDtypeStruct((Bs, 512), F32),
                   jax.ShapeDtypeStruct((Bs, 512), F32),
                   jax.ShapeDtypeStruct((POOL_BUF, Bs, 512), F32),
                   jax.ShapeDtypeStruct((Bs, A_HEADS, HEAD_W, HEAD_W), F32)],
        scratch_shapes=[pltpu.VMEM((ns, 512), F32)],
        compiler_params=_cparams(("parallel",)),
        name="sample_mix",
    )(zs, zs, zs, zs, zs, sp_l, s0_l, lbp, gh, w_pool, sc)


def kernel(x_prompt, x_sample, cache_k, cache_v, state_pool, state_hgrn, page_table, rel_table, lb_param, w_in, g_mix, g_q, g_k, lam_p, g_sub, w_pool, pool_scale, g_h, w_branch, w_out, g_mlp, w_up, w_down):
    B, T, D = x_prompt.shape
    Bs = x_sample.shape[0]
    depth, n_phys = cache_k.shape[:2]
    npages = page_table.shape[1]
    past_len = npages * PAGE
    tile = ATT_TILE

    lb_all = jnp.cumsum(jax.nn.softmax(lb_param.astype(F32), axis=0), axis=0)
    lb_all = lb_all - lb_all[:1]
    lbp_all = jnp.stack([jnp.log(lb_all), jnp.log1p(-lb_all), 1.0 - lb_all], axis=1)
    lp = lam_p.astype(F32)
    lam_dyn = jnp.exp(jnp.sum(lp[:, 0] * lp[:, 1], axis=-1)) - jnp.exp(jnp.sum(lp[:, 2] * lp[:, 3], axis=-1))
    w_in_b, w_pool_b, w_branch_b = w_in.astype(BF16), w_pool.astype(BF16), w_branch.astype(BF16)
    w_out_b, w_up_b, w_down_b = w_out.astype(BF16), w_up.astype(BF16), w_down.astype(BF16)

    rr = np.arange(tile)[:, None]
    cc = np.arange(2 * tile)[None, :]
    dist = rr + tile - cc
    bkt_prompt = np.where(dist >= 0, _rel_bucket_np(np.maximum(dist, 0)), -1).astype(np.int32)
    bias_prompt = _bias_tiles(rel_table, bkt_prompt, LOG2E)
    dd = np.concatenate([PAGE - np.arange(PAGE), np.zeros(PAGE, np.int64)])
    bkt_dec = np.broadcast_to(_rel_bucket_np(dd)[None, :], (8, 2 * PAGE)).astype(np.int32)
    bias_dec_h = _bias_tiles(rel_table, bkt_dec, 1.0)
    bias_dec = jnp.repeat(bias_dec_h[:, 0, :], 2, axis=0)

    tab2 = (rel_table.astype(F32) - rel_table[REL_BUCKETS - 1:].astype(F32)) * LOG2E
    bias_hi = jnp.maximum(jnp.max(tab2, axis=0), 0.0)
    bias_lo = jnp.minimum(jnp.min(tab2, axis=0), 0.0)
    slack = 1.01
    k_bound = 8.0 * slack * jnp.max(jnp.abs(g_k.astype(F32)), axis=-1)
    q_bound = 8.0 * slack * (A_QK_DIM ** -0.5 * LOG2E) * jnp.max(jnp.abs(g_q.astype(F32)), axis=-1)
    spread = 2.0 * q_bound * k_bound + jnp.max(bias_hi - bias_lo)
    bounded = (spread <= ATT_MAX_SPREAD).astype(F32)

    ckT = jnp.transpose(cache_k, (0, 1, 3, 4, 5, 2)).reshape(depth, n_phys, 512, PAGE)
    cv4 = cache_v.reshape(depth, n_phys, A_HEADS * PAGE, HEAD_W)
    sp_t = jnp.transpose(state_pool, (0, 2, 1, 3))
    pt_flat = page_table.reshape(-1).astype(jnp.int32)

    xp = x_prompt.reshape(B * T, D)
    xs = x_sample.reshape(Bs, D)
    kp_l, vp_l, ks_l, vs_l, pp_l, ps_l, sp_l, ss_l = [], [], [], [], [], [], [], []
    for l in range(depth):
        lam_init = 0.8 - 0.6 * math.exp(-0.3 * l)
        out_scale = 1.0 - lam_init
        lam = (lam_dyn[l] + lam_init).reshape(1).astype(F32)
        par = jnp.concatenate([lam, bounded[l].reshape(1), k_bound[l].reshape(1), bias_hi]).astype(F32)
        gmix = g_mix[l].reshape(1, D)
        gq_col = jnp.tile(g_q[l], 8).reshape(512, 1)
        gk_col = jnp.tile(g_k[l], 8).reshape(512, 1)
        gq2 = jnp.tile(g_q[l], 2).reshape(1, HEAD_W)
        gs = g_sub[l].reshape(1, HEAD_W)
        gh = g_h[l].reshape(1, HEAD_W)
        sc = pool_scale[l].reshape(1, 512)
        gmlp = g_mlp[l].reshape(1, D)
        lbp = lbp_all[l]

        z = _inproj(xp, gmix, w_in_b[l], tm=512)
        kT, kTb, v4, vb1 = _kvprep(z, gk_col, B, T)
        a = _attn_prompt(par, z, kTb, vb1, bias_prompt, gq2, gs, B, T, out_scale)
        p = _pool_prompt(z, w_pool_b[l], sc, B, T)
        o, st = _hgrn_prompt(z, lbp, gh, B, T)
        x1 = _merge(xp, a, p, o, z, w_branch_b[l], w_out_b[l], tm=512)
        xp = _mlp(x1, gmlp, w_up_b[l], w_down_b[l], tm=512)
        kp_l.append(kT)
        vp_l.append(v4)
        pp_l.append(z.reshape(B, T, -1)[:, T - POOL_BUF:, ZC_U * 512:(ZC_U + 1) * 512])
        sp_l.append(st)

        zs = _inproj(xs, gmix, w_in_b[l], tm=Bs)
        qn, kTs, ssf = _sprep(zs, gq_col, gk_col)
        a_s = _decode(pt_flat, lam, qn, ssf, zs, bias_dec, gs, ckT, cv4, l, npages, out_scale)
        o_s, p_s, spn, s1 = _smix(zs, sp_t[l], state_hgrn[l], lbp, gh, w_pool_b[l], sc, past_len)
        x1s = _merge(xs, a_s, p_s, o_s, zs, w_branch_b[l], w_out_b[l], tm=Bs)
        xs = _mlp(x1s, gmlp, w_up_b[l], w_down_b[l], tm=Bs)
        ks_l.append(kTs)
        vs_l.append(zs[:, ZC_V * 512:(ZC_V + 1) * 512])
        ps_l.append(spn)
        ss_l.append(s1)

    k_prompt = jnp.transpose(jnp.stack(kp_l).reshape(depth, B, A_HEADS, 2, A_QK_DIM, T), (0, 1, 5, 2, 3, 4))
    v_prompt = jnp.stack(vp_l).reshape(depth, B, T, A_HEADS, A_V_DIM)
    k_sample = jnp.transpose(jnp.stack(ks_l).reshape(depth, A_HEADS, 2, A_QK_DIM, Bs), (0, 4, 1, 2, 3))[:, :, None]
    v_sample = jnp.stack(vs_l).reshape(depth, Bs, 1, A_HEADS, A_V_DIM)
    pool_prompt = jnp.stack(pp_l)
    pool_sample = jnp.transpose(jnp.stack(ps_l), (0, 2, 1, 3))
    return (xp.reshape(B, T, D), xs.reshape(Bs, 1, D), k_prompt, v_prompt, k_sample, v_sample,
            pool_prompt, pool_sample, jnp.stack(sp_l), jnp.stack(ss_l))
```

```python
import functools
import math

import numpy as np
import jax
import jax.numpy as jnp
from jax import lax
from jax.experimental import pallas as pl
from jax.experimental.pallas import tpu as pltpu

F32 = jnp.float32
BF16 = jnp.bfloat16
EPS = 1e-6

A_HEADS = 4
A_QK_DIM = 64
A_V_DIM = 128
HEAD_W = 128
P_WINDOWS = (2, 4, 8, 16)
POOL_BUF = 15
N_BRANCH = 3
BRANCH_W = 512
REL_BUCKETS = 32
REL_MAX_DIST = 128
PAGE = 128
NEG = -1e30
LOG2E = math.log2(math.e)

ZC_Q, ZC_K, ZC_V, ZC_U, ZC_HQ, ZC_HF, ZC_HI, ZC_HG = range(8)
ZC_GZ = 8 * 512

VMEM_LIMIT = 48 * 1024 * 1024
ATT_VMEM_LIMIT = 56 * 1024 * 1024
ATT_TILE = 512
ATT_MAX_SPREAD = 100.0
HGRN_CHUNK = 64
HGRN_SUB = 8


def _cparams(sem):
    return pltpu.CompilerParams(dimension_semantics=sem, vmem_limit_bytes=VMEM_LIMIT)


def _resident(shape):
    nd = len(shape)
    return pl.BlockSpec(shape, lambda *_: (0,) * nd)


def _sigmoid(x):
    return 1.0 / (1.0 + jnp.exp(-x))


def _silu(x):
    return x * _sigmoid(x)


def _inproj_body(x_ref, g_ref, w_ref, z_ref):
    x = x_ref[...]
    ms = jnp.mean(x * x, axis=-1, keepdims=True)
    h = (x * lax.rsqrt(ms + EPS) * g_ref[...]).astype(BF16)
    z_ref[...] = jnp.dot(h, w_ref[...], preferred_element_type=F32)


def _inproj(x, g, w, layer, tm):
    M, D = x.shape
    N = w.shape[2]
    tn = N // 2
    return pl.pallas_call(
        _inproj_body,
        grid=(N // tn, M // tm),
        in_specs=[
            pl.BlockSpec((tm, D), lambda j, i: (i, 0)),
            pl.BlockSpec((1, D), lambda j, i: (0, 0)),
            pl.BlockSpec((None, D, tn), lambda j, i: (layer, 0, j), pipeline_mode=pl.Buffered(1)),
        ],
        out_specs=pl.BlockSpec((tm, tn), lambda j, i: (i, j)),
        out_shape=jax.ShapeDtypeStruct((M, N), F32),
        compiler_params=_cparams(("parallel", "parallel")),
        name="inproj",
    )(x, g, w)


def _group_norm_T(xT, gcol):
    n = xT.shape[1]
    x3 = xT.reshape(8, A_QK_DIM, n)
    ms = jnp.mean(x3 * x3, axis=1, keepdims=True)
    return (x3 * lax.rsqrt(ms + EPS)).reshape(8 * A_QK_DIM, n) * gcol


def _kvprep_body(k_ref, v_ref, gk_ref, kT_ref, kTb_ref, v4_ref, vb1_ref, *, tm):
    kn = _group_norm_T(k_ref[...].T, gk_ref[...])
    kT_ref[...] = kn
    kTb_ref[...] = kn.astype(BF16)
    v = v_ref[...]
    ones = jnp.ones((tm, HEAD_W), BF16)
    for h in range(A_HEADS):
        vh = v[:, h * HEAD_W:(h + 1) * HEAD_W]
        v4_ref[pl.ds(h, tm, stride=A_HEADS), :] = vh
        vb1_ref[:, 2 * h * HEAD_W:(2 * h + 1) * HEAD_W] = vh.astype(BF16)
        vb1_ref[:, (2 * h + 1) * HEAD_W:(2 * h + 2) * HEAD_W] = ones


def _kvprep(z, gk_col, B, T, tm=512):
    nt = T // tm
    M = B * T
    return pl.pallas_call(
        functools.partial(_kvprep_body, tm=tm),
        grid=(B, nt),
        in_specs=[
            pl.BlockSpec((tm, 512), lambda b, i: (b * nt + i, ZC_K)),
            pl.BlockSpec((tm, 512), lambda b, i: (b * nt + i, ZC_V)),
            pl.BlockSpec((512, 1), lambda b, i: (0, 0)),
        ],
        out_specs=[
            pl.BlockSpec((None, 512, tm), lambda b, i: (b, 0, i)),
            pl.BlockSpec((None, 512, tm), lambda b, i: (b, 0, i)),
            pl.BlockSpec((None, A_HEADS * tm, HEAD_W), lambda b, i: (b, i, 0)),
            pl.BlockSpec((tm, 2 * 512), lambda b, i: (b * nt + i, 0)),
        ],
        out_shape=[
            jax.ShapeDtypeStruct((B, 512, T), F32),
            jax.ShapeDtypeStruct((B, 512, T), BF16),
            jax.ShapeDtypeStruct((B, A_HEADS * T, HEAD_W), F32),
            jax.ShapeDtypeStruct((M, 2 * 512), BF16),
        ],
        compiler_params=_cparams(("parallel", "parallel")),
        name="kvprep",
    )(z, z, gk_col)


def _rel_bucket_np(n):
    n = np.asarray(n, np.int32)
    max_exact = REL_BUCKETS // 2
    nf = np.maximum(n, max_exact).astype(np.float32)
    large = max_exact + (np.log(nf / np.float32(max_exact)) / np.float32(math.log(REL_MAX_DIST / max_exact))
                         * np.float32(REL_BUCKETS - max_exact)).astype(np.int32)
    large = np.minimum(large, REL_BUCKETS - 1)
    return np.where(n < max_exact, n, large).astype(np.int32)


def _bias_body(tab_ref, bkt_ref, o_ref, *, scale):
    h = pl.program_id(0)
    bkt = bkt_ref[...]
    far = tab_ref[REL_BUCKETS - 1, h]
    acc = jnp.full(bkt.shape, NEG, F32)
    for b in range(REL_BUCKETS):
        acc = jnp.where(bkt == b, (tab_ref[b, h] - far) * scale, acc)
    o_ref[...] = acc


def _bias_tiles(rel_table, buckets, scale):
    R, C = buckets.shape
    return pl.pallas_call(
        functools.partial(_bias_body, scale=scale),
        grid=(A_HEADS,),
        in_specs=[pl.BlockSpec(memory_space=pltpu.SMEM), _resident((R, C))],
        out_specs=pl.BlockSpec((None, R, C), lambda h: (h, 0, 0)),
        out_shape=jax.ShapeDtypeStruct((A_HEADS, R, C), F32),
        compiler_params=_cparams(("arbitrary",)),
        name="bias_tiles",
    )(rel_table, jnp.asarray(buckets))


def _attn_body(pt_ref, par_ref, q_ref, kT_ref, v1_ref, bias_ref, gq_ref, gs_ref,
               qn_ref, ss_ref, vnew_ref, biasd_ref, *rest, tile, out_scale, npages, spg):
    del pt_ref
    k_refs = rest[:spg * npages]
    v_refs = rest[spg * npages:2 * spg * npages]
    o_ref, od_ref, mb_scr, m_scr, acc_scr = rest[2 * spg * npages:]
    h = pl.program_id(1)
    qi = pl.program_id(2)
    step_id = (pl.program_id(0) * pl.num_programs(1) + h) * pl.num_programs(2) + qi
    for j in range(spg):
        _decode_seq(step_id * spg + j, par_ref[0], qn_ref, ss_ref, vnew_ref, biasd_ref, gs_ref,
                    k_refs[j * npages:(j + 1) * npages], v_refs[j * npages:(j + 1) * npages],
                    od_ref, out_scale)

    q = q_ref[...]
    lane = lax.broadcasted_iota(jnp.int32, q.shape, 1)
    lo = lane < A_QK_DIM
    q2 = q * q
    ms0 = jnp.sum(jnp.where(lo, q2, 0.0), axis=-1, keepdims=True) * (1.0 / A_QK_DIM)
    ms1 = jnp.sum(jnp.where(lo, 0.0, q2), axis=-1, keepdims=True) * (1.0 / A_QK_DIM)
    rs = jnp.where(lo, lax.rsqrt(ms0 + EPS), lax.rsqrt(ms1 + EPS))
    qn = q * rs * gq_ref[...] * (A_QK_DIM ** -0.5 * LOG2E)
    qf = (jnp.where(lo, qn, 0.0), jnp.where(lo, 0.0, qn))
    qs = (qf[0].astype(BF16), qf[1].astype(BF16))
    acc_scr[...] = jnp.zeros(acc_scr.shape, F32)
    diag = pl.multiple_of(qi * tile, tile)
    sub = pl.multiple_of(jnp.maximum(qi - 1, 0) * tile, tile)

    n_far = jnp.maximum(qi - 1, 0)

    @pl.when(par_ref[1] > 0.5)
    def _():
        for mp in range(2):
            nq = jnp.sqrt(jnp.sum(qf[mp] * qf[mp], axis=-1, keepdims=True))
            mb_scr[mp] = jnp.broadcast_to(nq * par_ref[2] + par_ref[3 + h], (tile, tile))

        def pv(ks, bias):
            kt = kT_ref[:, pl.ds(ks, tile)]
            vt = v1_ref[pl.ds(ks, tile), :]
            out = []
            for mp in range(2):
                s = jnp.dot(qs[mp], kt, preferred_element_type=F32)
                if bias is not None:
                    s = s + bias
                p = jnp.exp2(s - mb_scr[mp]).astype(BF16)
                out.append(jnp.dot(p, vt, preferred_element_type=F32))
            return out

        def far_pair(i, carry):
            c0 = pv(pl.multiple_of(2 * i * tile, tile), None)
            c1 = pv(pl.multiple_of((2 * i + 1) * tile, tile), None)
            for mp in range(2):
                acc_scr[mp] = acc_scr[mp] + c0[mp] + c1[mp]
            return carry

        lax.fori_loop(0, n_far >> 1, far_pair, 0)

        @pl.when((n_far & 1) == 1)
        def _():
            c = pv(pl.multiple_of((n_far - 1) * tile, tile), None)
            for mp in range(2):
                acc_scr[mp] += c[mp]

        @pl.when(qi >= 1)
        def _():
            c = pv(sub, bias_ref[:, 0:tile])
            for mp in range(2):
                acc_scr[mp] += c[mp]

        c = pv(diag, bias_ref[:, tile:2 * tile])
        for mp in range(2):
            acc_scr[mp] += c[mp]

    @pl.when(par_ref[1] <= 0.5)
    def _():
        m_scr[...] = jnp.full(m_scr.shape, -jnp.inf, F32)

        def step(ks, bias):
            kt = kT_ref[:, pl.ds(ks, tile)]
            vt = v1_ref[pl.ds(ks, tile), :]
            for mp in range(2):
                s = jnp.dot(qs[mp], kt, preferred_element_type=F32)
                if bias is not None:
                    s = s + bias
                m_prev = m_scr[mp]
                m_new = jnp.maximum(m_prev, jnp.max(s, axis=-1, keepdims=True))
                p = jnp.exp2(s - m_new).astype(BF16)
                acc_scr[mp] = (jnp.exp2(m_prev - m_new) * acc_scr[mp]
                               + jnp.dot(p, vt, preferred_element_type=F32))
                m_scr[mp] = m_new

        def far_step(i, carry):
            step(pl.multiple_of(i * tile, tile), None)
            return carry

        lax.fori_loop(0, n_far, far_step, 0)

        @pl.when(qi >= 1)
        def _():
            step(sub, bias_ref[:, 0:tile])

        step(diag, bias_ref[:, tile:2 * tile])

    a0 = acc_scr[0]
    a1 = acc_scr[1]
    o = a0[:, :HEAD_W] / a0[:, HEAD_W:] - par_ref[0] * (a1[:, :HEAD_W] / a1[:, HEAD_W:])
    ms = jnp.mean(o * o, axis=-1, keepdims=True)
    o_ref[...] = (o * lax.rsqrt(ms + EPS) * gs_ref[...] * out_scale).astype(o_ref.dtype)


def _attn_prompt(pt_flat, par, z, kTb, vb1, bias, gq2, gs, qn, ss, zs, bias_dec, ckT, cv, layer,
                 B, T, out_scale, npages, spg, tile=ATT_TILE):
    nq = T // tile
    M = B * T
    Bs = qn.shape[0]
    once = pl.Buffered(1)

    def page_spec(j, p):
        def imap(b, h, i, pt):
            seq = ((b * A_HEADS + h) * nq + i) * spg + j
            return (layer, pt[seq * npages + p], 0, 0)
        return pl.BlockSpec((None, None, 512, PAGE), imap)

    pages = [page_spec(j, p) for j in range(spg) for p in range(npages)]
    in_specs = [
        pl.BlockSpec(memory_space=pltpu.SMEM),
        pl.BlockSpec((tile, HEAD_W), lambda b, h, i, pt: (b * nq + i, ZC_Q * 4 + h)),
        pl.BlockSpec((None, HEAD_W, T), lambda b, h, i, pt: (b, h, 0), pipeline_mode=once),
        pl.BlockSpec((T, 2 * HEAD_W), lambda b, h, i, pt: (b, h), pipeline_mode=once),
        pl.BlockSpec((None, tile, 2 * tile), lambda b, h, i, pt: (h, 0, 0), pipeline_mode=once),
        pl.BlockSpec((1, HEAD_W), lambda b, h, i, pt: (0, 0)),
        pl.BlockSpec((1, HEAD_W), lambda b, h, i, pt: (0, 0)),
        pl.BlockSpec((Bs, 512), lambda b, h, i, pt: (0, 0)),
        pl.BlockSpec((8, Bs), lambda b, h, i, pt: (0, 0)),
        pl.BlockSpec((Bs, 512), lambda b, h, i, pt: (0, ZC_V)),
        pl.BlockSpec((8, 2 * PAGE), lambda b, h, i, pt: (0, 0)),
    ] + pages + pages
    return pl.pallas_call(
        functools.partial(_attn_body, tile=tile, out_scale=out_scale, npages=npages, spg=spg),
        grid_spec=pltpu.PrefetchScalarGridSpec(
            num_scalar_prefetch=1,
            grid=(B, A_HEADS, nq),
            in_specs=in_specs,
            out_specs=[
                pl.BlockSpec((tile, HEAD_W), lambda b, h, i, pt: (b * nq + i, h)),
                pl.BlockSpec((Bs, 512), lambda b, h, i, pt: (0, 0)),
            ],
            scratch_shapes=[
                pltpu.VMEM((2, tile, tile), F32),
                pltpu.VMEM((2, tile, 1), F32),
                pltpu.VMEM((2, tile, 2 * HEAD_W), F32),
            ],
        ),
        out_shape=[jax.ShapeDtypeStruct((M, A_HEADS * HEAD_W), BF16),
                   jax.ShapeDtypeStruct((Bs, 512), F32)],
        compiler_params=pltpu.CompilerParams(
            dimension_semantics=("arbitrary", "arbitrary", "arbitrary"),
            vmem_limit_bytes=ATT_VMEM_LIMIT),
        name="attn_prompt",
    )(pt_flat, par, z, kTb, vb1, bias, gq2, gs, qn, ss, zs, bias_dec,
      *([ckT] * (spg * npages)), *([cv] * (spg * npages)))


def _pool_mix(ext_ref, base, n, pos, w_ref, sc_ref, o_ref):
    for g, win in enumerate(P_WINDOWS):
        sl = slice(g * HEAD_W, (g + 1) * HEAD_W)
        ug = ext_ref[base:base + n, sl]
        s = ug
        for j in range(1, win):
            s = s + ext_ref[base - j:base - j + n, sl]
        cnt = jnp.minimum(pos + 1, win).astype(F32)
        pooled = s / cnt - ug
        mixed = jnp.dot(pooled.astype(BF16), w_ref[g], preferred_element_type=F32) * sc_ref[:, sl]
        o_ref[:, sl] = mixed.astype(o_ref.dtype)


def _pool_body(u_ref, w_ref, sc_ref, o_ref, ext_scr, *, tm):
    i = pl.program_id(1)
    halo = 16

    @pl.when(i == 0)
    def _():
        ext_scr[0:halo, :] = jnp.zeros((halo, ext_scr.shape[1]), F32)

    @pl.when(i > 0)
    def _():
        ext_scr[0:halo, :] = ext_scr[tm:tm + halo, :]

    ext_scr[halo:halo + tm, :] = u_ref[...]
    pos = i * tm + lax.broadcasted_iota(jnp.int32, (tm, 1), 0)
    _pool_mix(ext_scr, halo, tm, pos, w_ref, sc_ref, o_ref)


def _pool_prompt(z, w_pool, sc, layer, B, T, tm=512):
    nt = T // tm
    return pl.pallas_call(
        functools.partial(_pool_body, tm=tm),
        grid=(B, nt),
        in_specs=[
            pl.BlockSpec((tm, 512), lambda b, i: (b * nt + i, ZC_U)),
            pl.BlockSpec((None, 4, HEAD_W, HEAD_W), lambda b, i: (layer, 0, 0, 0)),
            _resident((1, 512)),
        ],
        out_specs=pl.BlockSpec((tm, 512), lambda b, i: (b * nt + i, 0)),
        out_shape=jax.ShapeDtypeStruct((B * T, 512), BF16),
        scratch_shapes=[pltpu.VMEM((tm + 16, 512), F32)],
        compiler_params=_cparams(("parallel", "arbitrary")),
        name="pool_prompt",
    )(z, w_pool, sc)


def _hgrn_gates(f, log_lb, log1m_lb, one_m_lb):
    ls = jnp.minimum(f, 0.0) - jnp.log(1.0 + jnp.exp(-jnp.abs(f)))
    c = log1m_lb + ls
    mx = jnp.maximum(log_lb, c)
    lg = mx + jnp.log(1.0 + jnp.exp(-jnp.abs(log_lb - c)))
    kk = one_m_lb * _sigmoid(-f)
    return lg, kk


def _hgrn_consts(C):
    r = lax.broadcasted_iota(jnp.int32, (C, C), 0)
    c = lax.broadcasted_iota(jnp.int32, (C, C), 1)
    tri = (r >= c).astype(F32)
    rin = lax.broadcasted_iota(jnp.int32, (C, 1), 0) & (HGRN_SUB - 1)
    validf = [(rin >= d).astype(F32) for d in range(HGRN_SUB)]
    bdiff = (r >> 3) - (c >> 3)
    bandf = [(bdiff == d).astype(F32) for d in range(C // HGRN_SUB)]
    return tri, validf, bandf


def _hgrn_head(q, kk, v, b, ST, consts, C):
    _, validf, bandf = consts
    nb = C // HGRN_SUB
    b3 = b.reshape(nb, HGRN_SUB, HEAD_W)
    kk3 = kk.reshape(nb, HGRN_SUB, HEAD_W)
    v3 = v.reshape(nb, HGRN_SUB, HEAD_W)

    def back(x3, d):
        return pltpu.roll(x3, d, 1).reshape(C, HEAD_W)

    o = jnp.sum(q * kk, axis=-1, keepdims=True) * v
    for d in range(1, HGRN_SUB):
        e = jnp.exp2(jnp.minimum(b - back(b3, d), 0.0))
        a = jnp.sum(q * back(kk3, d) * e, axis=-1, keepdims=True)
        o = o + (a * validf[d]) * back(v3, d)

    bend = b3[:, HGRN_SUB - 1:HGRN_SUB, :]
    bstart = jnp.concatenate([jnp.zeros((1, 1, HEAD_W), F32), bend[:-1]], axis=0)
    qt = (q.reshape(nb, HGRN_SUB, HEAD_W) * jnp.exp2(b3 - bstart)).reshape(C, HEAD_W).astype(BF16)
    a_off = None
    for d in range(1, nb):
        bsh = jnp.concatenate([bend[d - 1:]] + [bend[nb - 1:nb]] * (d - 1), axis=0)
        kd = (kk3 * jnp.exp2(bsh - b3)).reshape(C, HEAD_W).astype(BF16)
        panel = lax.dot_general(qt, kd, (((1,), (1,)), ((), ())), preferred_element_type=F32)
        a_off = panel * bandf[d] if a_off is None else a_off + panel * bandf[d]
    vb = v.astype(BF16)
    o = o + jnp.dot(a_off.astype(BF16), vb, preferred_element_type=F32)

    qe = (q * jnp.exp2(b)).astype(BF16)
    o = o + lax.dot_general(qe, ST.astype(BF16), (((1,), (1,)), ((), ())), preferred_element_type=F32)
    b_end = b[C - 1:C, :]
    ke = (kk * jnp.exp2(b_end - b)).astype(BF16)
    ST_new = ST * jnp.exp2(b_end) + lax.dot_general(vb, ke, (((0,), (0,)), ((), ())),
                                                     preferred_element_type=F32)
    return o, ST_new


def _hgrn_body(q_ref, f_ref, i_ref, g_ref, lbp_ref, gh_ref, o_ref, st_ref, ST_scr, *, tb, C):
    it = pl.program_id(1)

    @pl.when(it == 0)
    def _():
        ST_scr[...] = jnp.zeros(ST_scr.shape, F32)

    lbp = lbp_ref[...]
    gh = gh_ref[...]
    consts = _hgrn_consts(C)

    def chunk(ci, carry):
        rows = pl.ds(pl.multiple_of(ci * C, C), C)
        q = _silu(q_ref[rows, :])
        lg, kk = _hgrn_gates(f_ref[rows, :], lbp[0:1, :], lbp[1:2, :], lbp[2:3, :])
        b = jnp.dot(consts[0], lg * LOG2E, preferred_element_type=F32, precision=lax.Precision.HIGHEST)
        v = i_ref[rows, :]
        gate = _silu(g_ref[rows, :])
        for h in range(A_HEADS):
            sl = slice(h * HEAD_W, (h + 1) * HEAD_W)
            o, ST_new = _hgrn_head(q[:, sl], kk[:, sl], v[:, sl], b[:, sl], ST_scr[h], consts, C)
            ST_scr[h] = ST_new
            ms = jnp.mean(o * o, axis=-1, keepdims=True)
            o_ref[rows, sl] = (o * lax.rsqrt(ms + EPS) * gh * gate[:, sl]).astype(o_ref.dtype)
        return carry

    lax.fori_loop(0, tb // C, chunk, 0)

    @pl.when(it == pl.num_programs(1) - 1)
    def _():
        for h in range(A_HEADS):
            st_ref[h] = ST_scr[h].T


def _hgrn_prompt(z, lbp, gh, B, T, tb=512, C=HGRN_CHUNK):
    nt = T // tb

    def zspec(cb):
        return pl.BlockSpec((tb, 512), lambda b, i: (b * nt + i, cb))

    return pl.pallas_call(
        functools.partial(_hgrn_body, tb=tb, C=C),
        grid=(B, nt),
        in_specs=[zspec(ZC_HQ), zspec(ZC_HF), zspec(ZC_HI), zspec(ZC_HG),
                  _resident((3, 512)), _resident((1, HEAD_W))],
        out_specs=[
            pl.BlockSpec((tb, 512), lambda b, i: (b * nt + i, 0)),
            pl.BlockSpec((None, A_HEADS, HEAD_W, HEAD_W), lambda b, i: (b, 0, 0, 0)),
        ],
        out_shape=[
            jax.ShapeDtypeStruct((B * T, 512), BF16),
            jax.ShapeDtypeStruct((B, A_HEADS, HEAD_W, HEAD_W), F32),
        ],
        scratch_shapes=[pltpu.VMEM((A_HEADS, HEAD_W, HEAD_W), F32)],
        compiler_params=_cparams(("parallel", "arbitrary")),
        name="hgrn_prompt",
    )(z, z, z, z, lbp, gh)


def _merge_body(x_ref, a_ref, p_ref, o_ref, g0_ref, g1_ref, g2_ref, wb_ref, wo_ref, y_ref):
    merged = None
    for n, (br, gz) in enumerate(((a_ref, g0_ref), (p_ref, g1_ref), (o_ref, g2_ref))):
        proj = jnp.dot(br[...].astype(BF16), wb_ref[n], preferred_element_type=F32)
        term = _sigmoid(gz[...]) * proj
        merged = term if merged is None else merged + term
    y_ref[...] = x_ref[...] + jnp.dot(merged.astype(BF16), wo_ref[...], preferred_element_type=F32)


def _merge(x, a, p, o, z, wb, wo, layer, tm):
    M, D = x.shape
    gcb = ZC_GZ // D

    def gspec(n):
        return pl.BlockSpec((tm, D), lambda i: (i, gcb + n))

    bspec = pl.BlockSpec((tm, BRANCH_W), lambda i: (i, 0))
    return pl.pallas_call(
        _merge_body,
        grid=(M // tm,),
        in_specs=[pl.BlockSpec((tm, D), lambda i: (i, 0)), bspec, bspec, bspec,
                  gspec(0), gspec(1), gspec(2),
                  pl.BlockSpec((None, N_BRANCH, BRANCH_W, D), lambda i: (layer, 0, 0, 0)),
                  pl.BlockSpec((None, D, D), lambda i: (layer, 0, 0))],
        out_specs=pl.BlockSpec((tm, D), lambda i: (i, 0)),
        out_shape=jax.ShapeDtypeStruct((M, D), F32),
        compiler_params=_cparams(("parallel",)),
        name="merge",
    )(x, a, p, o, z, z, z, wb, wo)


def _mlp_body(x_ref, g_ref, wu_ref, wd_ref, y_ref, *, fc):
    x = x_ref[...]
    ms = jnp.mean(x * x, axis=-1, keepdims=True)
    hm = (x * lax.rsqrt(ms + EPS) * g_ref[...]).astype(BF16)
    acc = x
    for c in range(wu_ref.shape[1] // fc):
        u = jnp.dot(hm, wu_ref[:, c * fc:(c + 1) * fc], preferred_element_type=F32)
        r = jnp.maximum(u, 0.0)
        acc = acc + jnp.dot((r * r).astype(BF16), wd_ref[c * fc:(c + 1) * fc, :], preferred_element_type=F32)
    y_ref[...] = acc


def _mlp(x, g, wu, wd, layer, tm, fc=1024):
    M, D = x.shape
    Fd = wu.shape[2]
    return pl.pallas_call(
        functools.partial(_mlp_body, fc=fc),
        grid=(M // tm,),
        in_specs=[pl.BlockSpec((tm, D), lambda i: (i, 0)), _resident((1, D)),
                  pl.BlockSpec((None, D, Fd), lambda i: (layer, 0, 0), pipeline_mode=pl.Buffered(1)),
                  pl.BlockSpec((None, Fd, D), lambda i: (layer, 0, 0), pipeline_mode=pl.Buffered(1))],
        out_specs=pl.BlockSpec((tm, D), lambda i: (i, 0)),
        out_shape=jax.ShapeDtypeStruct((M, D), F32),
        compiler_params=_cparams(("parallel",)),
        name="mlp",
    )(x, g, wu, wd)


def _sprep_body(q_ref, k_ref, gq_ref, gk_ref, qn_ref, kT_ref, ss_ref):
    qnT = _group_norm_T(q_ref[...].T, gq_ref[...]) * (A_QK_DIM ** -0.5)
    knT = _group_norm_T(k_ref[...].T, gk_ref[...])
    qn_ref[...] = qnT.T
    kT_ref[...] = knT
    ss_ref[...] = jnp.sum((qnT * knT).reshape(8, A_QK_DIM, qnT.shape[1]), axis=1)


def _sprep(zs, gq_col, gk_col):
    Bs = zs.shape[0]
    return pl.pallas_call(
        _sprep_body,
        grid=(1,),
        in_specs=[pl.BlockSpec((Bs, 512), lambda i: (0, ZC_Q)),
                  pl.BlockSpec((Bs, 512), lambda i: (0, ZC_K)),
                  _resident((512, 1)), _resident((512, 1))],
        out_specs=[_resident((Bs, 512)), _resident((512, Bs)), _resident((8, Bs))],
        out_shape=[jax.ShapeDtypeStruct((Bs, 512), F32),
                   jax.ShapeDtypeStruct((512, Bs), F32),
                   jax.ShapeDtypeStruct((8, Bs), F32)],
        compiler_params=_cparams(("arbitrary",)),
        name="sample_prep",
    )(zs, zs, gq_col, gk_col)


def _decode_body(pt_ref, lam_ref, qn_ref, ss_ref, v_ref, bias_ref, gs_ref, *rest, npages, out_scale):
    del pt_ref
    _decode_seq(pl.program_id(0), lam_ref[0], qn_ref, ss_ref, v_ref, bias_ref, gs_ref,
                rest[:npages], rest[npages:2 * npages], rest[2 * npages], out_scale)


def _decode_seq(b, lam, qn_ref, ss_ref, v_ref, bias_ref, gs_ref, k_refs, v_refs, o_ref, out_scale):
    npages = len(k_refs)
    Bs = qn_ref.shape[0]

    q_row = qn_ref[pl.ds(b, 1), :]
    row8 = lax.broadcasted_iota(jnp.int32, (8, 512), 0)
    grp = lax.broadcasted_iota(jnp.int32, (8, 512), 1) >> 6
    wq = jnp.where(grp == row8, jnp.broadcast_to(q_row, (8, 512)), 0.0).astype(BF16)

    s_pages = [jnp.dot(wq, k_refs[p][...].astype(BF16), preferred_element_type=F32)
               for p in range(npages)]
    s_pages[-1] = s_pages[-1] + bias_ref[:, 0:PAGE]
    lane_b = lax.broadcasted_iota(jnp.int32, (8, Bs), 1)
    s_self = (jnp.sum(jnp.where(lane_b == b, ss_ref[...], 0.0), axis=-1, keepdims=True)
              + bias_ref[:, PAGE:PAGE + 1])

    m = s_self
    for s in s_pages:
        m = jnp.maximum(m, jnp.max(s, axis=-1, keepdims=True))
    p_self = jnp.exp(s_self - m)
    p_pages = [jnp.exp(s - m) for s in s_pages]
    l = p_self
    for p in p_pages:
        l = l + jnp.sum(p, axis=-1, keepdims=True)
    row1 = lax.broadcasted_iota(jnp.int32, (8, 1), 0)
    coef = jnp.where((row1 & 1) == 0, 1.0, -lam) / l

    def pair(x):
        return x + pltpu.roll(x, 7, 0)

    w_self = pair(jnp.broadcast_to(p_self * coef, (8, HEAD_W)))
    accs = [jnp.zeros((8, HEAD_W), F32) for _ in range(A_HEADS)]
    for pg in range(npages):
        wp = pair(p_pages[pg] * coef).astype(BF16)
        for h in range(A_HEADS):
            vh = v_refs[pg][pl.ds(h, PAGE, stride=A_HEADS), :].astype(BF16)
            accs[h] = accs[h] + jnp.dot(wp, vh, preferred_element_type=F32)
    v_new = v_ref[pl.ds(b, 1), :]
    outs = []
    for h in range(A_HEADS):
        oh = accs[h][2 * h:2 * h + 1, :] + w_self[2 * h:2 * h + 1, :] * v_new[:, h * HEAD_W:(h + 1) * HEAD_W]
        ms = jnp.mean(oh * oh, axis=-1, keepdims=True)
        outs.append(oh * lax.rsqrt(ms + EPS) * gs_ref[...] * out_scale)
    o_ref[pl.ds(b, 1), :] = jnp.concatenate(outs, axis=1)


def _decode(pt_flat, lam, qn, ss, zs, bias_dec, gs, ckT, cv, layer, npages, out_scale):
    Bs = qn.shape[0]

    def page_spec(p):
        return pl.BlockSpec((None, None, 512, PAGE),
                            lambda b, pt, p=p: (layer, pt[b * npages + p], 0, 0))

    in_specs = [
        pl.BlockSpec(memory_space=pltpu.SMEM),
        pl.BlockSpec((Bs, 512), lambda b, pt: (0, 0)),
        pl.BlockSpec((8, Bs), lambda b, pt: (0, 0)),
        pl.BlockSpec((Bs, 512), lambda b, pt: (0, ZC_V)),
        pl.BlockSpec((8, 2 * PAGE), lambda b, pt: (0, 0)),
        pl.BlockSpec((1, HEAD_W), lambda b, pt: (0, 0)),
    ] + [page_spec(p) for p in range(npages)] * 2
    return pl.pallas_call(
        functools.partial(_decode_body, npages=npages, out_scale=out_scale),
        grid_spec=pltpu.PrefetchScalarGridSpec(
            num_scalar_prefetch=1,
            grid=(Bs,),
            in_specs=in_specs,
            out_specs=pl.BlockSpec((Bs, 512), lambda b, pt: (0, 0)),
        ),
        out_shape=jax.ShapeDtypeStruct((Bs, 512), F32),
        compiler_params=_cparams(("arbitrary",)),
        name="decode_attn",
    )(pt_flat, lam, qn, ss, zs, bias_dec, gs, *([ckT] * npages), *([cv] * npages))


def _smix_body(q_ref, f_ref, i_ref, g_ref, u_ref, sp_ref, s0_ref, lbp_ref, gh_ref, wp_ref, sc_ref,
               o_ref, p_ref, spn_ref, s1_ref, o_scr, *, ns, past_len):
    q = _silu(q_ref[...])
    lg, kk = _hgrn_gates(f_ref[...], lbp_ref[0:1, :], lbp_ref[1:2, :], lbp_ref[2:3, :])
    g = jnp.exp(lg)
    v = i_ref[...]
    eye = (lax.broadcasted_iota(jnp.int32, (HEAD_W, HEAD_W), 0)
           == lax.broadcasted_iota(jnp.int32, (HEAD_W, HEAD_W), 1))

    def col_of(row):
        return jnp.sum(jnp.where(eye, jnp.broadcast_to(row, eye.shape), 0.0), axis=-1, keepdims=True)

    for s in range(ns):
        for h in range(A_HEADS):
            sl = slice(h * HEAD_W, (h + 1) * HEAD_W)
            S_new = (col_of(g[s:s + 1, sl]) * s0_ref[s, h]
                     + col_of(kk[s:s + 1, sl]) * v[s:s + 1, sl])
            s1_ref[s, h] = S_new
            o_scr[s:s + 1, sl] = jnp.sum(col_of(q[s:s + 1, sl]) * S_new, axis=0, keepdims=True)
    zg = g_ref[...]
    for h in range(A_HEADS):
        sl = slice(h * HEAD_W, (h + 1) * HEAD_W)
        oh = o_scr[:, sl]
        ms = jnp.mean(oh * oh, axis=-1, keepdims=True)
        o_ref[:, sl] = oh * lax.rsqrt(ms + EPS) * gh_ref[...] * _silu(zg[:, sl])

    u = u_ref[...]
    for g_i, win in enumerate(P_WINDOWS):
        sl = slice(g_i * HEAD_W, (g_i + 1) * HEAD_W)
        ug = u[:, sl]
        sacc = ug
        for j in range(1, win):
            sacc = sacc + sp_ref[POOL_BUF - j][:, sl]
        cnt = float(min(past_len + 1, win))
        pooled = sacc / cnt - ug
        p_ref[:, sl] = (jnp.dot(pooled.astype(BF16), wp_ref[g_i], preferred_element_type=F32)
                        * sc_ref[:, sl])
    for j in range(POOL_BUF - 1):
        spn_ref[j] = sp_ref[j + 1]
    spn_ref[POOL_BUF - 1] = u


def _smix(zs, sp_t, s0, lbp, gh, w_pool, sc, layer, past_len, ns=8):
    Bs = zs.shape[0]

    def zspec(cb):
        return pl.BlockSpec((ns, 512), lambda i: (i, cb))

    return pl.pallas_call(
        functools.partial(_smix_body, ns=ns, past_len=past_len),
        grid=(Bs // ns,),
        in_specs=[zspec(ZC_HQ), zspec(ZC_HF), zspec(ZC_HI), zspec(ZC_HG), zspec(ZC_U),
                  pl.BlockSpec((None, POOL_BUF, ns, 512), lambda i: (layer, 0, i, 0)),
                  pl.BlockSpec((None, ns, A_HEADS, HEAD_W, HEAD_W), lambda i: (layer, i, 0, 0, 0)),
                  _resident((3, 512)), _resident((1, HEAD_W)),
                  pl.BlockSpec((None, 4, HEAD_W, HEAD_W), lambda i: (layer, 0, 0, 0)),
                  _resident((1, 512))],
        out_specs=[pl.BlockSpec((ns, 512), lambda i: (i, 0)),
                   pl.BlockSpec((ns, 512), lambda i: (i, 0)),
                   pl.BlockSpec((POOL_BUF, ns, 512), lambda i: (0, i, 0)),
                   pl.BlockSpec((ns, A_HEADS, HEAD_W, HEAD_W), lambda i: (i, 0, 0, 0))],
        out_shape=[jax.ShapeDtypeStruct((Bs, 512), F32),
                   jax.ShapeDtypeStruct((Bs, 512), F32),
                   jax.ShapeDtypeStruct((POOL_BUF, Bs, 512), F32),
                   jax.ShapeDtypeStruct((Bs, A_HEADS, HEAD_W, HEAD_W), F32)],
        scratch_shapes=[pltpu.VMEM((ns, 512), F32)],
        compiler_params=_cparams(("parallel",)),
        name="sample_mix",
    )(zs, zs, zs, zs, zs, sp_t, s0, lbp, gh, w_pool, sc)


def kernel(x_prompt, x_sample, cache_k, cache_v, state_pool, state_hgrn, page_table, rel_table, lb_param, w_in, g_mix, g_q, g_k, lam_p, g_sub, w_pool, pool_scale, g_h, w_branch, w_out, g_mlp, w_up, w_down):
    B, T, D = x_prompt.shape
    Bs = x_sample.shape[0]
    depth, n_phys = cache_k.shape[:2]
    npages = page_table.shape[1]
    past_len = npages * PAGE
    tile = ATT_TILE
    att_steps = B * A_HEADS * (T // tile)
    spg = Bs // att_steps if Bs % att_steps == 0 else 0

    lb_all = jnp.cumsum(jax.nn.softmax(lb_param.astype(F32), axis=0), axis=0)
    lb_all = lb_all - lb_all[:1]
    lbp_all = jnp.stack([jnp.log(lb_all), jnp.log1p(-lb_all), 1.0 - lb_all], axis=1)
    lp = lam_p.astype(F32)
    lam_dyn = jnp.exp(jnp.sum(lp[:, 0] * lp[:, 1], axis=-1)) - jnp.exp(jnp.sum(lp[:, 2] * lp[:, 3], axis=-1))
    w_in_b, w_pool_b, w_branch_b = w_in.astype(BF16), w_pool.astype(BF16), w_branch.astype(BF16)
    w_out_b, w_up_b, w_down_b = w_out.astype(BF16), w_up.astype(BF16), w_down.astype(BF16)

    rr = np.arange(tile)[:, None]
    cc = np.arange(2 * tile)[None, :]
    dist = rr + tile - cc
    bkt_prompt = np.where(dist >= 0, _rel_bucket_np(np.maximum(dist, 0)), -1).astype(np.int32)
    bias_prompt = _bias_tiles(rel_table, bkt_prompt, LOG2E)
    dd = np.concatenate([PAGE - np.arange(PAGE), np.zeros(PAGE, np.int64)])
    bkt_dec = np.broadcast_to(_rel_bucket_np(dd)[None, :], (8, 2 * PAGE)).astype(np.int32)
    bias_dec_h = _bias_tiles(rel_table, bkt_dec, 1.0)
    bias_dec = jnp.repeat(bias_dec_h[:, 0, :], 2, axis=0)

    tab2 = (rel_table.astype(F32) - rel_table[REL_BUCKETS - 1:].astype(F32)) * LOG2E
    bias_hi = jnp.maximum(jnp.max(tab2, axis=0), 0.0)
    bias_lo = jnp.minimum(jnp.min(tab2, axis=0), 0.0)
    slack = 1.01
    k_bound = 8.0 * slack * jnp.max(jnp.abs(g_k.astype(F32)), axis=-1)
    q_bound = 8.0 * slack * (A_QK_DIM ** -0.5 * LOG2E) * jnp.max(jnp.abs(g_q.astype(F32)), axis=-1)
    spread = 2.0 * q_bound * k_bound + jnp.max(bias_hi - bias_lo)
    bounded = (spread <= ATT_MAX_SPREAD).astype(F32)

    ckT = jnp.transpose(cache_k, (0, 1, 3, 4, 5, 2)).reshape(depth, n_phys, 512, PAGE)
    cv4 = cache_v.reshape(depth, n_phys, A_HEADS * PAGE, HEAD_W)
    sp_t = jnp.transpose(state_pool, (0, 2, 1, 3))
    pt_flat = page_table.reshape(-1).astype(jnp.int32)

    xp = x_prompt.reshape(B * T, D)
    xs = x_sample.reshape(Bs, D)
    kp_l, vp_l, ks_l, vs_l, pp_l, ps_l, sp_l, ss_l = [], [], [], [], [], [], [], []
    for l in range(depth):
        lam_init = 0.8 - 0.6 * math.exp(-0.3 * l)
        out_scale = 1.0 - lam_init
        lam = (lam_dyn[l] + lam_init).reshape(1).astype(F32)
        par = jnp.concatenate([lam, bounded[l].reshape(1), k_bound[l].reshape(1), bias_hi]).astype(F32)
        gmix = g_mix[l].reshape(1, D)
        gq_col = jnp.tile(g_q[l], 8).reshape(512, 1)
        gk_col = jnp.tile(g_k[l], 8).reshape(512, 1)
        gq2 = jnp.tile(g_q[l], 2).reshape(1, HEAD_W)
        gs = g_sub[l].reshape(1, HEAD_W)
        gh = g_h[l].reshape(1, HEAD_W)
        sc = pool_scale[l].reshape(1, 512)
        gmlp = g_mlp[l].reshape(1, D)
        lbp = lbp_all[l]

        zs = _inproj(xs, gmix, w_in_b, l, tm=Bs)
        qn, kTs, ssf = _sprep(zs, gq_col, gk_col)
        z = _inproj(xp, gmix, w_in_b, l, tm=512)
        kT, kTb, v4, vb1 = _kvprep(z, gk_col, B, T)

        a, a_s = _attn_prompt(pt_flat, par, z, kTb, vb1, bias_prompt, gq2, gs, qn, ssf, zs, bias_dec,
                              ckT, cv4, l, B, T, out_scale, npages, spg)
        if spg == 0:
            a_s = _decode(pt_flat, lam, qn, ssf, zs, bias_dec, gs, ckT, cv4, l, npages, out_scale)

        p = _pool_prompt(z, w_pool_b, sc, l, B, T)
        o, st = _hgrn_prompt(z, lbp, gh, B, T)
        x1 = _merge(xp, a, p, o, z, w_branch_b, w_out_b, l, tm=512)
        xp = _mlp(x1, gmlp, w_up_b, w_down_b, l, tm=512)
        kp_l.append(kT)
        vp_l.append(v4)
        pp_l.append(z.reshape(B, T, -1)[:, T - POOL_BUF:, ZC_U * 512:(ZC_U + 1) * 512])
        sp_l.append(st)

        o_s, p_s, spn, s1 = _smix(zs, sp_t, state_hgrn, lbp, gh, w_pool_b, sc, l, past_len)
        x1s = _merge(xs, a_s, p_s, o_s, zs, w_branch_b, w_out_b, l, tm=Bs)
        xs = _mlp(x1s, gmlp, w_up_b, w_down_b, l, tm=Bs)
        ks_l.append(kTs)
        vs_l.append(zs[:, ZC_V * 512:(ZC_V + 1) * 512])
        ps_l.append(spn)
        ss_l.append(s1)

    k_prompt = jnp.transpose(jnp.stack(kp_l).reshape(depth, B, A_HEADS, 2, A_QK_DIM, T), (0, 1, 5, 2, 3, 4))
    v_prompt = jnp.stack(vp_l).reshape(depth, B, T, A_HEADS, A_V_DIM)
    k_sample = jnp.transpose(jnp.stack(ks_l).reshape(depth, A_HEADS, 2, A_QK_DIM, Bs), (0, 4, 1, 2, 3))[:, :, None]
    v_sample = jnp.stack(vs_l).reshape(depth, Bs, 1, A_HEADS, A_V_DIM)
    pool_prompt = jnp.stack(pp_l)
    pool_sample = jnp.transpose(jnp.stack(ps_l), (0, 2, 1, 3))
    return (xp.reshape(B, T, D), xs.reshape(Bs, 1, D), k_prompt, v_prompt, k_sample, v_sample,
            pool_prompt, pool_sample, jnp.stack(sp_l), jnp.stack(ss_l))
```

```python
import functools
import math

import numpy as np
import jax
import jax.numpy as jnp
from jax import lax
from jax.experimental import pallas as pl
from jax.experimental.pallas import tpu as pltpu

F32 = jnp.float32
BF16 = jnp.bfloat16
EPS = 1e-6

A_HEADS = 4
A_QK_DIM = 64
A_V_DIM = 128
HEAD_W = 128
P_WINDOWS = (2, 4, 8, 16)
POOL_BUF = 15
N_BRANCH = 3
BRANCH_W = 512
REL_BUCKETS = 32
REL_MAX_DIST = 128
PAGE = 128
NEG = -1e30
LOG2E = math.log2(math.e)

ZC_Q, ZC_K, ZC_V, ZC_U, ZC_HQ, ZC_HF, ZC_HI, ZC_HG = range(8)
ZC_GZ = 8 * 512

VMEM_LIMIT = 48 * 1024 * 1024
ATT_VMEM_LIMIT = 56 * 1024 * 1024
ATT_TILE = 512
ATT_MAX_SPREAD = 100.0
HGRN_CHUNK = 64
HGRN_SUB = 8
HGRN_HEAD_GROUPS = ((0, 1, 2, 3),)
HGRN_UNROLL = 4


def _cparams(sem):
    return pltpu.CompilerParams(dimension_semantics=sem, vmem_limit_bytes=VMEM_LIMIT)


def _resident(shape):
    nd = len(shape)
    return pl.BlockSpec(shape, lambda *_: (0,) * nd)


def _sigmoid(x):
    return 1.0 / (1.0 + jnp.exp(-x))


def _silu(x):
    return x * _sigmoid(x)


def _inproj_body(x_ref, g_ref, w_ref, z_ref):
    x = x_ref[...]
    ms = jnp.mean(x * x, axis=-1, keepdims=True)
    h = (x * lax.rsqrt(ms + EPS) * g_ref[...]).astype(BF16)
    z_ref[...] = jnp.dot(h, w_ref[...], preferred_element_type=F32)


def _inproj(x, g, w, layer, tm):
    M, D = x.shape
    N = w.shape[2]
    tn = N // 2
    return pl.pallas_call(
        _inproj_body,
        grid=(N // tn, M // tm),
        in_specs=[
            pl.BlockSpec((tm, D), lambda j, i: (i, 0)),
            pl.BlockSpec((1, D), lambda j, i: (0, 0)),
            pl.BlockSpec((None, D, tn), lambda j, i: (layer, 0, j), pipeline_mode=pl.Buffered(1)),
        ],
        out_specs=pl.BlockSpec((tm, tn), lambda j, i: (i, j)),
        out_shape=jax.ShapeDtypeStruct((M, N), F32),
        compiler_params=_cparams(("parallel", "parallel")),
        name="inproj",
    )(x, g, w)


def _group_norm_T(xT, gcol):
    n = xT.shape[1]
    x3 = xT.reshape(8, A_QK_DIM, n)
    ms = jnp.mean(x3 * x3, axis=1, keepdims=True)
    return (x3 * lax.rsqrt(ms + EPS)).reshape(8 * A_QK_DIM, n) * gcol


def _kvprep_body(k_ref, v_ref, gk_ref, kT_ref, kTb_ref, v4_ref, vb1_ref, *, tm):
    kn = _group_norm_T(k_ref[...].T, gk_ref[...])
    kT_ref[...] = kn
    kTb_ref[...] = kn.astype(BF16)
    v = v_ref[...]
    ones = jnp.ones((tm, HEAD_W), BF16)
    for h in range(A_HEADS):
        vh = v[:, h * HEAD_W:(h + 1) * HEAD_W]
        v4_ref[pl.ds(h, tm, stride=A_HEADS), :] = vh
        vb1_ref[:, 2 * h * HEAD_W:(2 * h + 1) * HEAD_W] = vh.astype(BF16)
        vb1_ref[:, (2 * h + 1) * HEAD_W:(2 * h + 2) * HEAD_W] = ones


def _kvprep(z, gk_col, B, T, tm=512):
    nt = T // tm
    M = B * T
    return pl.pallas_call(
        functools.partial(_kvprep_body, tm=tm),
        grid=(B, nt),
        in_specs=[
            pl.BlockSpec((tm, 512), lambda b, i: (b * nt + i, ZC_K)),
            pl.BlockSpec((tm, 512), lambda b, i: (b * nt + i, ZC_V)),
            pl.BlockSpec((512, 1), lambda b, i: (0, 0)),
        ],
        out_specs=[
            pl.BlockSpec((None, 512, tm), lambda b, i: (b, 0, i)),
            pl.BlockSpec((None, 512, tm), lambda b, i: (b, 0, i)),
            pl.BlockSpec((None, A_HEADS * tm, HEAD_W), lambda b, i: (b, i, 0)),
            pl.BlockSpec((tm, 2 * 512), lambda b, i: (b * nt + i, 0)),
        ],
        out_shape=[
            jax.ShapeDtypeStruct((B, 512, T), F32),
            jax.ShapeDtypeStruct((B, 512, T), BF16),
            jax.ShapeDtypeStruct((B, A_HEADS * T, HEAD_W), F32),
            jax.ShapeDtypeStruct((M, 2 * 512), BF16),
        ],
        compiler_params=_cparams(("parallel", "parallel")),
        name="kvprep",
    )(z, z, gk_col)


def _rel_bucket_np(n):
    n = np.asarray(n, np.int32)
    max_exact = REL_BUCKETS // 2
    nf = np.maximum(n, max_exact).astype(np.float32)
    large = max_exact + (np.log(nf / np.float32(max_exact)) / np.float32(math.log(REL_MAX_DIST / max_exact))
                         * np.float32(REL_BUCKETS - max_exact)).astype(np.int32)
    large = np.minimum(large, REL_BUCKETS - 1)
    return np.where(n < max_exact, n, large).astype(np.int32)


def _bias_body(tab_ref, bkt_ref, o_ref, *, scale):
    h = pl.program_id(0)
    bkt = bkt_ref[...]
    far = tab_ref[REL_BUCKETS - 1, h]
    acc = jnp.full(bkt.shape, NEG, F32)
    for b in range(REL_BUCKETS):
        acc = jnp.where(bkt == b, (tab_ref[b, h] - far) * scale, acc)
    o_ref[...] = acc


def _bias_tiles(rel_table, buckets, scale):
    R, C = buckets.shape
    return pl.pallas_call(
        functools.partial(_bias_body, scale=scale),
        grid=(A_HEADS,),
        in_specs=[pl.BlockSpec(memory_space=pltpu.SMEM), _resident((R, C))],
        out_specs=pl.BlockSpec((None, R, C), lambda h: (h, 0, 0)),
        out_shape=jax.ShapeDtypeStruct((A_HEADS, R, C), F32),
        compiler_params=_cparams(("arbitrary",)),
        name="bias_tiles",
    )(rel_table, jnp.asarray(buckets))


def _attn_body(pt_ref, par_ref, q_ref, kT_ref, v1_ref, bias_ref, gq_ref, gs_ref,
               qn_ref, ss_ref, vnew_ref, biasd_ref, *rest, tile, out_scale, npages, spg):
    del pt_ref
    k_refs = rest[:spg * npages]
    v_refs = rest[spg * npages:2 * spg * npages]
    o_ref, od_ref, mb_scr, m_scr, acc_scr = rest[2 * spg * npages:]
    h = pl.program_id(1)
    qi = pl.program_id(2)
    step_id = (pl.program_id(0) * pl.num_programs(1) + h) * pl.num_programs(2) + qi

    def decode():
        for j in range(spg):
            _decode_seq(step_id * spg + j, par_ref[0], qn_ref, ss_ref, vnew_ref, biasd_ref, gs_ref,
                        k_refs[j * npages:(j + 1) * npages], v_refs[j * npages:(j + 1) * npages],
                        od_ref, out_scale)

    q = q_ref[...]
    lane = lax.broadcasted_iota(jnp.int32, q.shape, 1)
    lo = lane < A_QK_DIM
    q2 = q * q
    ms0 = jnp.sum(jnp.where(lo, q2, 0.0), axis=-1, keepdims=True) * (1.0 / A_QK_DIM)
    ms1 = jnp.sum(jnp.where(lo, 0.0, q2), axis=-1, keepdims=True) * (1.0 / A_QK_DIM)
    rs = jnp.where(lo, lax.rsqrt(ms0 + EPS), lax.rsqrt(ms1 + EPS))
    qn = q * rs * gq_ref[...] * (A_QK_DIM ** -0.5 * LOG2E)
    qf = (jnp.where(lo, qn, 0.0), jnp.where(lo, 0.0, qn))
    qs = (qf[0].astype(BF16), qf[1].astype(BF16))
    acc_scr[...] = jnp.zeros(acc_scr.shape, F32)
    diag = pl.multiple_of(qi * tile, tile)
    sub = pl.multiple_of(jnp.maximum(qi - 1, 0) * tile, tile)

    n_far = jnp.maximum(qi - 1, 0)

    @pl.when(par_ref[1] > 0.5)
    def _():
        for mp in range(2):
            nq = jnp.sqrt(jnp.sum(qf[mp] * qf[mp], axis=-1, keepdims=True))
            mb_scr[mp] = jnp.broadcast_to(nq * par_ref[2] + par_ref[3 + h], (tile, tile))

        def pv(ks, bias):
            kt = kT_ref[:, pl.ds(ks, tile)]
            vt = v1_ref[pl.ds(ks, tile), :]
            out = []
            for mp in range(2):
                s = jnp.dot(qs[mp], kt, preferred_element_type=F32)
                if bias is not None:
                    s = s + bias
                p = jnp.exp2(s - mb_scr[mp]).astype(BF16)
                out.append(jnp.dot(p, vt, preferred_element_type=F32))
            return out

        def far_pair(i, carry):
            c0 = pv(pl.multiple_of(2 * i * tile, tile), None)
            c1 = pv(pl.multiple_of((2 * i + 1) * tile, tile), None)
            for mp in range(2):
                acc_scr[mp] = acc_scr[mp] + c0[mp] + c1[mp]
            return carry

        lax.fori_loop(0, n_far >> 1, far_pair, 0)

        @pl.when((n_far & 1) == 1)
        def _():
            c = pv(pl.multiple_of((n_far - 1) * tile, tile), None)
            for mp in range(2):
                acc_scr[mp] += c[mp]

        @pl.when(qi >= 1)
        def _():
            c = pv(sub, bias_ref[:, 0:tile])
            for mp in range(2):
                acc_scr[mp] += c[mp]

        c = pv(diag, bias_ref[:, tile:2 * tile])
        decode()
        for mp in range(2):
            acc_scr[mp] += c[mp]

    @pl.when(par_ref[1] <= 0.5)
    def _():
        m_scr[...] = jnp.full(m_scr.shape, -jnp.inf, F32)

        def step(ks, bias):
            kt = kT_ref[:, pl.ds(ks, tile)]
            vt = v1_ref[pl.ds(ks, tile), :]
            for mp in range(2):
                s = jnp.dot(qs[mp], kt, preferred_element_type=F32)
                if bias is not None:
                    s = s + bias
                m_prev = m_scr[mp]
                m_new = jnp.maximum(m_prev, jnp.max(s, axis=-1, keepdims=True))
                p = jnp.exp2(s - m_new).astype(BF16)
                acc_scr[mp] = (jnp.exp2(m_prev - m_new) * acc_scr[mp]
                               + jnp.dot(p, vt, preferred_element_type=F32))
                m_scr[mp] = m_new

        def far_step(i, carry):
            step(pl.multiple_of(i * tile, tile), None)
            return carry

        lax.fori_loop(0, n_far, far_step, 0)

        @pl.when(qi >= 1)
        def _():
            step(sub, bias_ref[:, 0:tile])

        step(diag, bias_ref[:, tile:2 * tile])
        decode()

    a0 = acc_scr[0]
    a1 = acc_scr[1]
    o = a0[:, :HEAD_W] / a0[:, HEAD_W:] - par_ref[0] * (a1[:, :HEAD_W] / a1[:, HEAD_W:])
    ms = jnp.mean(o * o, axis=-1, keepdims=True)
    o_ref[...] = (o * lax.rsqrt(ms + EPS) * gs_ref[...] * out_scale).astype(o_ref.dtype)


def _attn_prompt(pt_flat, par, z, kTb, vb1, bias, gq2, gs, qn, ss, zs, bias_dec, ckT, cv, layer,
                 B, T, out_scale, npages, spg, tile=ATT_TILE):
    nq = T // tile
    M = B * T
    Bs = qn.shape[0]
    once = pl.Buffered(1)

    def page_spec(j, p):
        def imap(b, h, i, pt):
            seq = ((b * A_HEADS + h) * nq + i) * spg + j
            return (layer, pt[seq * npages + p], 0, 0)
        return pl.BlockSpec((None, None, 512, PAGE), imap)

    pages = [page_spec(j, p) for j in range(spg) for p in range(npages)]
    in_specs = [
        pl.BlockSpec(memory_space=pltpu.SMEM),
        pl.BlockSpec((tile, HEAD_W), lambda b, h, i, pt: (b * nq + i, ZC_Q * 4 + h)),
        pl.BlockSpec((None, HEAD_W, T), lambda b, h, i, pt: (b, h, 0), pipeline_mode=once),
        pl.BlockSpec((T, 2 * HEAD_W), lambda b, h, i, pt: (b, h), pipeline_mode=once),
        pl.BlockSpec((None, tile, 2 * tile), lambda b, h, i, pt: (h, 0, 0), pipeline_mode=once),
        pl.BlockSpec((1, HEAD_W), lambda b, h, i, pt: (0, 0)),
        pl.BlockSpec((1, HEAD_W), lambda b, h, i, pt: (0, 0)),
        pl.BlockSpec((Bs, 512), lambda b, h, i, pt: (0, 0)),
        pl.BlockSpec((8, Bs), lambda b, h, i, pt: (0, 0)),
        pl.BlockSpec((Bs, 512), lambda b, h, i, pt: (0, ZC_V)),
        pl.BlockSpec((8, 2 * PAGE), lambda b, h, i, pt: (0, 0)),
    ] + pages + pages
    return pl.pallas_call(
        functools.partial(_attn_body, tile=tile, out_scale=out_scale, npages=npages, spg=spg),
        grid_spec=pltpu.PrefetchScalarGridSpec(
            num_scalar_prefetch=1,
            grid=(B, A_HEADS, nq),
            in_specs=in_specs,
            out_specs=[
                pl.BlockSpec((tile, HEAD_W), lambda b, h, i, pt: (b * nq + i, h)),
                pl.BlockSpec((Bs, 512), lambda b, h, i, pt: (0, 0)),
            ],
            scratch_shapes=[
                pltpu.VMEM((2, tile, tile), F32),
                pltpu.VMEM((2, tile, 1), F32),
                pltpu.VMEM((2, tile, 2 * HEAD_W), F32),
            ],
        ),
        out_shape=[jax.ShapeDtypeStruct((M, A_HEADS * HEAD_W), BF16),
                   jax.ShapeDtypeStruct((Bs, 512), F32)],
        compiler_params=pltpu.CompilerParams(
            dimension_semantics=("arbitrary", "arbitrary", "arbitrary"),
            vmem_limit_bytes=ATT_VMEM_LIMIT),
        name="attn_prompt",
    )(pt_flat, par, z, kTb, vb1, bias, gq2, gs, qn, ss, zs, bias_dec,
      *([ckT] * (spg * npages)), *([cv] * (spg * npages)))


def _pool_mix(ext_ref, base, n, pos, w_ref, sc_ref, o_ref):
    for g, win in enumerate(P_WINDOWS):
        sl = slice(g * HEAD_W, (g + 1) * HEAD_W)
        ug = ext_ref[base:base + n, sl]
        s = ug
        for j in range(1, win):
            s = s + ext_ref[base - j:base - j + n, sl]
        cnt = jnp.minimum(pos + 1, win).astype(F32)
        pooled = s / cnt - ug
        mixed = jnp.dot(pooled.astype(BF16), w_ref[g], preferred_element_type=F32) * sc_ref[:, sl]
        o_ref[:, sl] = mixed.astype(o_ref.dtype)


def _pool_body(u_ref, w_ref, sc_ref, o_ref, ext_scr, *, tm):
    i = pl.program_id(1)
    halo = 16

    @pl.when(i == 0)
    def _():
        ext_scr[0:halo, :] = jnp.zeros((halo, ext_scr.shape[1]), F32)

    @pl.when(i > 0)
    def _():
        ext_scr[0:halo, :] = ext_scr[tm:tm + halo, :]

    ext_scr[halo:halo + tm, :] = u_ref[...]
    pos = i * tm + lax.broadcasted_iota(jnp.int32, (tm, 1), 0)
    _pool_mix(ext_scr, halo, tm, pos, w_ref, sc_ref, o_ref)


def _pool_prompt(z, w_pool, sc, layer, B, T, tm=512):
    nt = T // tm
    return pl.pallas_call(
        functools.partial(_pool_body, tm=tm),
        grid=(B, nt),
        in_specs=[
            pl.BlockSpec((tm, 512), lambda b, i: (b * nt + i, ZC_U)),
            pl.BlockSpec((None, 4, HEAD_W, HEAD_W), lambda b, i: (layer, 0, 0, 0)),
            _resident((1, 512)),
        ],
        out_specs=pl.BlockSpec((tm, 512), lambda b, i: (b * nt + i, 0)),
        out_shape=jax.ShapeDtypeStruct((B * T, 512), BF16),
        scratch_shapes=[pltpu.VMEM((tm + 16, 512), F32)],
        compiler_params=_cparams(("parallel", "arbitrary")),
        name="pool_prompt",
    )(z, w_pool, sc)


def _hgrn_gates(f, log_lb, log1m_lb, one_m_lb):
    ls = jnp.minimum(f, 0.0) - jnp.log(1.0 + jnp.exp(-jnp.abs(f)))
    c = log1m_lb + ls
    mx = jnp.maximum(log_lb, c)
    lg = mx + jnp.log(1.0 + jnp.exp(-jnp.abs(log_lb - c)))
    kk = one_m_lb * _sigmoid(-f)
    return lg, kk


def _hgrn_consts(C):
    r = lax.broadcasted_iota(jnp.int32, (C, C), 0)
    c = lax.broadcasted_iota(jnp.int32, (C, C), 1)
    tri = (r >= c).astype(F32)
    rin = lax.broadcasted_iota(jnp.int32, (C, 1), 0) & (HGRN_SUB - 1)
    validf = [(rin >= d).astype(F32) for d in range(HGRN_SUB)]
    bdiff = (r >> 3) - (c >> 3)
    bandf = [(bdiff == d).astype(F32) for d in range(C // HGRN_SUB)]
    return tri, validf, bandf


def _hgrn_head(q, kk, v, b, ST, consts, C):
    _, validf, bandf = consts
    nb = C // HGRN_SUB
    b3 = b.reshape(nb, HGRN_SUB, HEAD_W)
    kk3 = kk.reshape(nb, HGRN_SUB, HEAD_W)
    v3 = v.reshape(nb, HGRN_SUB, HEAD_W)

    def back(x3, d):
        return pltpu.roll(x3, d, 1).reshape(C, HEAD_W)

    o = jnp.sum(q * kk, axis=-1, keepdims=True) * v
    for d in range(1, HGRN_SUB):
        e = jnp.exp2(jnp.minimum(b - back(b3, d), 0.0))
        a = jnp.sum(q * back(kk3, d) * e, axis=-1, keepdims=True)
        o = o + (a * validf[d]) * back(v3, d)

    bend = b3[:, HGRN_SUB - 1:HGRN_SUB, :]
    bstart = jnp.concatenate([jnp.zeros((1, 1, HEAD_W), F32), bend[:-1]], axis=0)
    qt = (q.reshape(nb, HGRN_SUB, HEAD_W) * jnp.exp2(b3 - bstart)).reshape(C, HEAD_W).astype(BF16)
    a_off = None
    for d in range(1, nb):
        bsh = jnp.concatenate([bend[d - 1:]] + [bend[nb - 1:nb]] * (d - 1), axis=0)
        kd = (kk3 * jnp.exp2(bsh - b3)).reshape(C, HEAD_W).astype(BF16)
        panel = lax.dot_general(qt, kd, (((1,), (1,)), ((), ())), preferred_element_type=F32)
        a_off = panel * bandf[d] if a_off is None else a_off + panel * bandf[d]
    vb = v.astype(BF16)
    o = o + jnp.dot(a_off.astype(BF16), vb, preferred_element_type=F32)

    qe = (q * jnp.exp2(b)).astype(BF16)
    o = o + lax.dot_general(qe, ST.astype(BF16), (((1,), (1,)), ((), ())), preferred_element_type=F32)
    b_end = b[C - 1:C, :]
    ke = (kk * jnp.exp2(b_end - b)).astype(BF16)
    ST_new = ST * jnp.exp2(b_end) + lax.dot_general(vb, ke, (((0,), (0,)), ((), ())),
                                                     preferred_element_type=F32)
    return o, ST_new


def _hgrn_body(q_ref, f_ref, i_ref, g_ref, lbp_ref, gh_ref, o_ref, st_ref, ST_scr, *, tb, C):
    it = pl.program_id(1)

    @pl.when(it == 0)
    def _():
        ST_scr[...] = jnp.zeros(ST_scr.shape, F32)

    lbp = lbp_ref[...]
    gh = gh_ref[...]
    consts = _hgrn_consts(C)

    def chunk_of(heads):
        w = slice(heads[0] * HEAD_W, (heads[-1] + 1) * HEAD_W)

        def chunk(ci, carry):
            rows = pl.ds(pl.multiple_of(ci * C, C), C)
            q = _silu(q_ref[rows, w])
            lg, kk = _hgrn_gates(f_ref[rows, w], lbp[0:1, w], lbp[1:2, w], lbp[2:3, w])
            b = jnp.dot(consts[0], lg * LOG2E, preferred_element_type=F32, precision=lax.Precision.HIGHEST)
            v = i_ref[rows, w]
            gate = _silu(g_ref[rows, w])
            for n, h in enumerate(heads):
                sl = slice(n * HEAD_W, (n + 1) * HEAD_W)
                o, ST_new = _hgrn_head(q[:, sl], kk[:, sl], v[:, sl], b[:, sl], ST_scr[h], consts, C)
                ST_scr[h] = ST_new
                ms = jnp.mean(o * o, axis=-1, keepdims=True)
                o_ref[rows, h * HEAD_W:(h + 1) * HEAD_W] = (
                    o * lax.rsqrt(ms + EPS) * gh * gate[:, sl]).astype(o_ref.dtype)
            return carry

        return chunk

    for heads in HGRN_HEAD_GROUPS:
        lax.fori_loop(0, tb // C, chunk_of(heads), 0, unroll=HGRN_UNROLL)

    @pl.when(it == pl.num_programs(1) - 1)
    def _():
        for h in range(A_HEADS):
            st_ref[h] = ST_scr[h].T


def _hgrn_prompt(z, lbp, gh, B, T, tb=512, C=HGRN_CHUNK):
    nt = T // tb

    def zspec(cb):
        return pl.BlockSpec((tb, 512), lambda b, i: (b * nt + i, cb))

    return pl.pallas_call(
        functools.partial(_hgrn_body, tb=tb, C=C),
        grid=(B, nt),
        in_specs=[zspec(ZC_HQ), zspec(ZC_HF), zspec(ZC_HI), zspec(ZC_HG),
                  _resident((3, 512)), _resident((1, HEAD_W))],
        out_specs=[
            pl.BlockSpec((tb, 512), lambda b, i: (b * nt + i, 0)),
            pl.BlockSpec((None, A_HEADS, HEAD_W, HEAD_W), lambda b, i: (b, 0, 0, 0)),
        ],
        out_shape=[
            jax.ShapeDtypeStruct((B * T, 512), BF16),
            jax.ShapeDtypeStruct((B, A_HEADS, HEAD_W, HEAD_W), F32),
        ],
        scratch_shapes=[pltpu.VMEM((A_HEADS, HEAD_W, HEAD_W), F32)],
        compiler_params=_cparams(("parallel", "arbitrary")),
        name="hgrn_prompt",
    )(z, z, z, z, lbp, gh)


def _merge_body(x_ref, a_ref, p_ref, o_ref, g0_ref, g1_ref, g2_ref, wb_ref, wo_ref, y_ref):
    merged = None
    for n, (br, gz) in enumerate(((a_ref, g0_ref), (p_ref, g1_ref), (o_ref, g2_ref))):
        proj = jnp.dot(br[...].astype(BF16), wb_ref[n], preferred_element_type=F32)
        term = _sigmoid(gz[...]) * proj
        merged = term if merged is None else merged + term
    y_ref[...] = x_ref[...] + jnp.dot(merged.astype(BF16), wo_ref[...], preferred_element_type=F32)


def _merge(x, a, p, o, z, wb, wo, layer, tm):
    M, D = x.shape
    gcb = ZC_GZ // D

    def gspec(n):
        return pl.BlockSpec((tm, D), lambda i: (i, gcb + n))

    bspec = pl.BlockSpec((tm, BRANCH_W), lambda i: (i, 0))
    return pl.pallas_call(
        _merge_body,
        grid=(M // tm,),
        in_specs=[pl.BlockSpec((tm, D), lambda i: (i, 0)), bspec, bspec, bspec,
                  gspec(0), gspec(1), gspec(2),
                  pl.BlockSpec((None, N_BRANCH, BRANCH_W, D), lambda i: (layer, 0, 0, 0)),
                  pl.BlockSpec((None, D, D), lambda i: (layer, 0, 0))],
        out_specs=pl.BlockSpec((tm, D), lambda i: (i, 0)),
        out_shape=jax.ShapeDtypeStruct((M, D), F32),
        compiler_params=_cparams(("parallel",)),
        name="merge",
    )(x, a, p, o, z, z, z, wb, wo)


def _mlp_body(x_ref, g_ref, wu_ref, wd_ref, y_ref, *, fc):
    x = x_ref[...]
    ms = jnp.mean(x * x, axis=-1, keepdims=True)
    hm = (x * lax.rsqrt(ms + EPS) * g_ref[...]).astype(BF16)
    acc = x
    for c in range(wu_ref.shape[1] // fc):
        u = jnp.dot(hm, wu_ref[:, c * fc:(c + 1) * fc], preferred_element_type=F32)
        r = jnp.maximum(u, 0.0)
        acc = acc + jnp.dot((r * r).astype(BF16), wd_ref[c * fc:(c + 1) * fc, :], preferred_element_type=F32)
    y_ref[...] = acc


def _mlp(x, g, wu, wd, layer, tm, fc=1024):
    M, D = x.shape
    Fd = wu.shape[2]
    return pl.pallas_call(
        functools.partial(_mlp_body, fc=fc),
        grid=(M // tm,),
        in_specs=[pl.BlockSpec((tm, D), lambda i: (i, 0)), _resident((1, D)),
                  pl.BlockSpec((None, D, Fd), lambda i: (layer, 0, 0), pipeline_mode=pl.Buffered(1)),
                  pl.BlockSpec((None, Fd, D), lambda i: (layer, 0, 0), pipeline_mode=pl.Buffered(1))],
        out_specs=pl.BlockSpec((tm, D), lambda i: (i, 0)),
        out_shape=jax.ShapeDtypeStruct((M, D), F32),
        compiler_params=_cparams(("parallel",)),
        name="mlp",
    )(x, g, wu, wd)


def _sprep_body(q_ref, k_ref, gq_ref, gk_ref, qn_ref, kT_ref, ss_ref):
    qnT = _group_norm_T(q_ref[...].T, gq_ref[...]) * (A_QK_DIM ** -0.5)
    knT = _group_norm_T(k_ref[...].T, gk_ref[...])
    qn_ref[...] = qnT.T
    kT_ref[...] = knT
    ss_ref[...] = jnp.sum((qnT * knT).reshape(8, A_QK_DIM, qnT.shape[1]), axis=1)


def _sprep(zs, gq_col, gk_col):
    Bs = zs.shape[0]
    return pl.pallas_call(
        _sprep_body,
        grid=(1,),
        in_specs=[pl.BlockSpec((Bs, 512), lambda i: (0, ZC_Q)),
                  pl.BlockSpec((Bs, 512), lambda i: (0, ZC_K)),
                  _resident((512, 1)), _resident((512, 1))],
        out_specs=[_resident((Bs, 512)), _resident((512, Bs)), _resident((8, Bs))],
        out_shape=[jax.ShapeDtypeStruct((Bs, 512), F32),
                   jax.ShapeDtypeStruct((512, Bs), F32),
                   jax.ShapeDtypeStruct((8, Bs), F32)],
        compiler_params=_cparams(("arbitrary",)),
        name="sample_prep",
    )(zs, zs, gq_col, gk_col)


def _decode_body(pt_ref, lam_ref, qn_ref, ss_ref, v_ref, bias_ref, gs_ref, *rest, npages, out_scale):
    del pt_ref
    _decode_seq(pl.program_id(0), lam_ref[0], qn_ref, ss_ref, v_ref, bias_ref, gs_ref,
                rest[:npages], rest[npages:2 * npages], rest[2 * npages], out_scale)


def _decode_seq(b, lam, qn_ref, ss_ref, v_ref, bias_ref, gs_ref, k_refs, v_refs, o_ref, out_scale):
    npages = len(k_refs)
    Bs = qn_ref.shape[0]

    q_row = qn_ref[pl.ds(b, 1), :]
    row8 = lax.broadcasted_iota(jnp.int32, (8, 512), 0)
    grp = lax.broadcasted_iota(jnp.int32, (8, 512), 1) >> 6
    wq = jnp.where(grp == row8, jnp.broadcast_to(q_row, (8, 512)), 0.0).astype(BF16)

    s_pages = [jnp.dot(wq, k_refs[p][...].astype(BF16), preferred_element_type=F32)
               for p in range(npages)]
    s_pages[-1] = s_pages[-1] + bias_ref[:, 0:PAGE]
    lane_b = lax.broadcasted_iota(jnp.int32, (8, Bs), 1)
    s_self = (jnp.sum(jnp.where(lane_b == b, ss_ref[...], 0.0), axis=-1, keepdims=True)
              + bias_ref[:, PAGE:PAGE + 1])

    m = s_self
    for s in s_pages:
        m = jnp.maximum(m, jnp.max(s, axis=-1, keepdims=True))
    p_self = jnp.exp(s_self - m)
    p_pages = [jnp.exp(s - m) for s in s_pages]
    l = p_self
    for p in p_pages:
        l = l + jnp.sum(p, axis=-1, keepdims=True)
    row1 = lax.broadcasted_iota(jnp.int32, (8, 1), 0)
    coef = jnp.where((row1 & 1) == 0, 1.0, -lam) / l

    def pair(x):
        return x + pltpu.roll(x, 7, 0)

    w_self = pair(jnp.broadcast_to(p_self * coef, (8, HEAD_W)))
    accs = [jnp.zeros((8, HEAD_W), F32) for _ in range(A_HEADS)]
    for pg in range(npages):
        wp = pair(p_pages[pg] * coef).astype(BF16)
        for h in range(A_HEADS):
            vh = v_refs[pg][pl.ds(h, PAGE, stride=A_HEADS), :].astype(BF16)
            accs[h] = accs[h] + jnp.dot(wp, vh, preferred_element_type=F32)
    v_new = v_ref[pl.ds(b, 1), :]
    outs = []
    for h in range(A_HEADS):
        oh = accs[h][2 * h:2 * h + 1, :] + w_self[2 * h:2 * h + 1, :] * v_new[:, h * HEAD_W:(h + 1) * HEAD_W]
        ms = jnp.mean(oh * oh, axis=-1, keepdims=True)
        outs.append(oh * lax.rsqrt(ms + EPS) * gs_ref[...] * out_scale)
    o_ref[pl.ds(b, 1), :] = jnp.concatenate(outs, axis=1)


def _decode(pt_flat, lam, qn, ss, zs, bias_dec, gs, ckT, cv, layer, npages, out_scale):
    Bs = qn.shape[0]

    def page_spec(p):
        return pl.BlockSpec((None, None, 512, PAGE),
                            lambda b, pt, p=p: (layer, pt[b * npages + p], 0, 0))

    in_specs = [
        pl.BlockSpec(memory_space=pltpu.SMEM),
        pl.BlockSpec((Bs, 512), lambda b, pt: (0, 0)),
        pl.BlockSpec((8, Bs), lambda b, pt: (0, 0)),
        pl.BlockSpec((Bs, 512), lambda b, pt: (0, ZC_V)),
        pl.BlockSpec((8, 2 * PAGE), lambda b, pt: (0, 0)),
        pl.BlockSpec((1, HEAD_W), lambda b, pt: (0, 0)),
    ] + [page_spec(p) for p in range(npages)] * 2
    return pl.pallas_call(
        functools.partial(_decode_body, npages=npages, out_scale=out_scale),
        grid_spec=pltpu.PrefetchScalarGridSpec(
            num_scalar_prefetch=1,
            grid=(Bs,),
            in_specs=in_specs,
            out_specs=pl.BlockSpec((Bs, 512), lambda b, pt: (0, 0)),
        ),
        out_shape=jax.ShapeDtypeStruct((Bs, 512), F32),
        compiler_params=_cparams(("arbitrary",)),
        name="decode_attn",
    )(pt_flat, lam, qn, ss, zs, bias_dec, gs, *([ckT] * npages), *([cv] * npages))


def _smix_body(q_ref, f_ref, i_ref, g_ref, u_ref, sp_ref, s0_ref, lbp_ref, gh_ref, wp_ref, sc_ref,
               o_ref, p_ref, spn_ref, s1_ref, o_scr, *, ns, past_len):
    q = _silu(q_ref[...])
    lg, kk = _hgrn_gates(f_ref[...], lbp_ref[0:1, :], lbp_ref[1:2, :], lbp_ref[2:3, :])
    g = jnp.exp(lg)
    v = i_ref[...]
    eye = (lax.broadcasted_iota(jnp.int32, (HEAD_W, HEAD_W), 0)
           == lax.broadcasted_iota(jnp.int32, (HEAD_W, HEAD_W), 1))

    def col_of(row):
        return jnp.sum(jnp.where(eye, jnp.broadcast_to(row, eye.shape), 0.0), axis=-1, keepdims=True)

    for s in range(ns):
        for h in range(A_HEADS):
            sl = slice(h * HEAD_W, (h + 1) * HEAD_W)
            S_new = (col_of(g[s:s + 1, sl]) * s0_ref[s, h]
                     + col_of(kk[s:s + 1, sl]) * v[s:s + 1, sl])
            s1_ref[s, h] = S_new
            o_scr[s:s + 1, sl] = jnp.sum(col_of(q[s:s + 1, sl]) * S_new, axis=0, keepdims=True)
    zg = g_ref[...]
    for h in range(A_HEADS):
        sl = slice(h * HEAD_W, (h + 1) * HEAD_W)
        oh = o_scr[:, sl]
        ms = jnp.mean(oh * oh, axis=-1, keepdims=True)
        o_ref[:, sl] = oh * lax.rsqrt(ms + EPS) * gh_ref[...] * _silu(zg[:, sl])

    u = u_ref[...]
    for g_i, win in enumerate(P_WINDOWS):
        sl = slice(g_i * HEAD_W, (g_i + 1) * HEAD_W)
        ug = u[:, sl]
        sacc = ug
        for j in range(1, win):
            sacc = sacc + sp_ref[POOL_BUF - j][:, sl]
        cnt = float(min(past_len + 1, win))
        pooled = sacc / cnt - ug
        p_ref[:, sl] = (jnp.dot(pooled.astype(BF16), wp_ref[g_i], preferred_element_type=F32)
                        * sc_ref[:, sl])
    for j in range(POOL_BUF - 1):
        spn_ref[j] = sp_ref[j + 1]
    spn_ref[POOL_BUF - 1] = u


def _smix(zs, sp_t, s0, lbp, gh, w_pool, sc, layer, past_len, ns=8):
    Bs = zs.shape[0]

    def zspec(cb):
        return pl.BlockSpec((ns, 512), lambda i: (i, cb))

    return pl.pallas_call(
        functools.partial(_smix_body, ns=ns, past_len=past_len),
        grid=(Bs // ns,),
        in_specs=[zspec(ZC_HQ), zspec(ZC_HF), zspec(ZC_HI), zspec(ZC_HG), zspec(ZC_U),
                  pl.BlockSpec((None, POOL_BUF, ns, 512), lambda i: (layer, 0, i, 0)),
                  pl.BlockSpec((None, ns, A_HEADS, HEAD_W, HEAD_W), lambda i: (layer, i, 0, 0, 0)),
                  _resident((3, 512)), _resident((1, HEAD_W)),
                  pl.BlockSpec((None, 4, HEAD_W, HEAD_W), lambda i: (layer, 0, 0, 0)),
                  _resident((1, 512))],
        out_specs=[pl.BlockSpec((ns, 512), lambda i: (i, 0)),
                   pl.BlockSpec((ns, 512), lambda i: (i, 0)),
                   pl.BlockSpec((POOL_BUF, ns, 512), lambda i: (0, i, 0)),
                   pl.BlockSpec((ns, A_HEADS, HEAD_W, HEAD_W), lambda i: (i, 0, 0, 0))],
        out_shape=[jax.ShapeDtypeStruct((Bs, 512), F32),
                   jax.ShapeDtypeStruct((Bs, 512), F32),
                   jax.ShapeDtypeStruct((POOL_BUF, Bs, 512), F32),
                   jax.ShapeDtypeStruct((Bs, A_HEADS, HEAD_W, HEAD_W), F32)],
        scratch_shapes=[pltpu.VMEM((ns, 512), F32)],
        compiler_params=_cparams(("parallel",)),
        name="sample_mix",
    )(zs, zs, zs, zs, zs, sp_t, s0, lbp, gh, w_pool, sc)


def kernel(x_prompt, x_sample, cache_k, cache_v, state_pool, state_hgrn, page_table, rel_table, lb_param, w_in, g_mix, g_q, g_k, lam_p, g_sub, w_pool, pool_scale, g_h, w_branch, w_out, g_mlp, w_up, w_down):
    B, T, D = x_prompt.shape
    Bs = x_sample.shape[0]
    depth, n_phys = cache_k.shape[:2]
    npages = page_table.shape[1]
    past_len = npages * PAGE
    tile = ATT_TILE
    att_steps = B * A_HEADS * (T // tile)
    spg = Bs // att_steps if Bs % att_steps == 0 else 0

    lb_all = jnp.cumsum(jax.nn.softmax(lb_param.astype(F32), axis=0), axis=0)
    lb_all = lb_all - lb_all[:1]
    lbp_all = jnp.stack([jnp.log(lb_all), jnp.log1p(-lb_all), 1.0 - lb_all], axis=1)
    lp = lam_p.astype(F32)
    lam_dyn = jnp.exp(jnp.sum(lp[:, 0] * lp[:, 1], axis=-1)) - jnp.exp(jnp.sum(lp[:, 2] * lp[:, 3], axis=-1))
    w_in_b, w_pool_b, w_branch_b = w_in.astype(BF16), w_pool.astype(BF16), w_branch.astype(BF16)
    w_out_b, w_up_b, w_down_b = w_out.astype(BF16), w_up.astype(BF16), w_down.astype(BF16)

    rr = np.arange(tile)[:, None]
    cc = np.arange(2 * tile)[None, :]
    dist = rr + tile - cc
    bkt_prompt = np.where(dist >= 0, _rel_bucket_np(np.maximum(dist, 0)), -1).astype(np.int32)
    bias_prompt = _bias_tiles(rel_table, bkt_prompt, LOG2E)
    dd = np.concatenate([PAGE - np.arange(PAGE), np.zeros(PAGE, np.int64)])
    bkt_dec = np.broadcast_to(_rel_bucket_np(dd)[None, :], (8, 2 * PAGE)).astype(np.int32)
    bias_dec_h = _bias_tiles(rel_table, bkt_dec, 1.0)
    bias_dec = jnp.repeat(bias_dec_h[:, 0, :], 2, axis=0)

    tab2 = (rel_table.astype(F32) - rel_table[REL_BUCKETS - 1:].astype(F32)) * LOG2E
    bias_hi = jnp.maximum(jnp.max(tab2, axis=0), 0.0)
    bias_lo = jnp.minimum(jnp.min(tab2, axis=0), 0.0)
    slack = 1.01
    k_bound = 8.0 * slack * jnp.max(jnp.abs(g_k.astype(F32)), axis=-1)
    q_bound = 8.0 * slack * (A_QK_DIM ** -0.5 * LOG2E) * jnp.max(jnp.abs(g_q.astype(F32)), axis=-1)
    spread = 2.0 * q_bound * k_bound + jnp.max(bias_hi - bias_lo)
    bounded = (spread <= ATT_MAX_SPREAD).astype(F32)

    ckT = jnp.transpose(cache_k, (0, 1, 3, 4, 5, 2)).reshape(depth, n_phys, 512, PAGE)
    cv4 = cache_v.reshape(depth, n_phys, A_HEADS * PAGE, HEAD_W)
    sp_t = jnp.transpose(state_pool, (0, 2, 1, 3))
    pt_flat = page_table.reshape(-1).astype(jnp.int32)

    xp = x_prompt.reshape(B * T, D)
    xs = x_sample.reshape(Bs, D)
    kp_l, vp_l, ks_l, vs_l, pp_l, ps_l, sp_l, ss_l = [], [], [], [], [], [], [], []
    for l in range(depth):
        lam_init = 0.8 - 0.6 * math.exp(-0.3 * l)
        out_scale = 1.0 - lam_init
        lam = (lam_dyn[l] + lam_init).reshape(1).astype(F32)
        par = jnp.concatenate([lam, bounded[l].reshape(1), k_bound[l].reshape(1), bias_hi]).astype(F32)
        gmix = g_mix[l].reshape(1, D)
        gq_col = jnp.tile(g_q[l], 8).reshape(512, 1)
        gk_col = jnp.tile(g_k[l], 8).reshape(512, 1)
        gq2 = jnp.tile(g_q[l], 2).reshape(1, HEAD_W)
        gs = g_sub[l].reshape(1, HEAD_W)
        gh = g_h[l].reshape(1, HEAD_W)
        sc = pool_scale[l].reshape(1, 512)
        gmlp = g_mlp[l].reshape(1, D)
        lbp = lbp_all[l]

        zs = _inproj(xs, gmix, w_in_b, l, tm=Bs)
        qn, kTs, ssf = _sprep(zs, gq_col, gk_col)
        z = _inproj(xp, gmix, w_in_b, l, tm=512)
        kT, kTb, v4, vb1 = _kvprep(z, gk_col, B, T)

        a, a_s = _attn_prompt(pt_flat, par, z, kTb, vb1, bias_prompt, gq2, gs, qn, ssf, zs, bias_dec,
                              ckT, cv4, l, B, T, out_scale, npages, spg)
        if spg == 0:
            a_s = _decode(pt_flat, lam, qn, ssf, zs, bias_dec, gs, ckT, cv4, l, npages, out_scale)

        p = _pool_prompt(z, w_pool_b, sc, l, B, T)
        o, st = _hgrn_prompt(z, lbp, gh, B, T)
        x1 = _merge(xp, a, p, o, z, w_branch_b, w_out_b, l, tm=512)
        xp = _mlp(x1, gmlp, w_up_b, w_down_b, l, tm=512)
        kp_l.append(kT)
        vp_l.append(v4)
        pp_l.append(z.reshape(B, T, -1)[:, T - POOL_BUF:, ZC_U * 512:(ZC_U + 1) * 512])
        sp_l.append(st)

        o_s, p_s, spn, s1 = _smix(zs, sp_t, state_hgrn, lbp, gh, w_pool_b, sc, l, past_len)
        x1s = _merge(xs, a_s, p_s, o_s, zs, w_branch_b, w_out_b, l, tm=Bs)
        xs = _mlp(x1s, gmlp, w_up_b, w_down_b, l, tm=Bs)
        ks_l.append(kTs)
        vs_l.append(zs[:, ZC_V * 512:(ZC_V + 1) * 512])
        ps_l.append(spn)
        ss_l.append(s1)

    k_prompt = jnp.transpose(jnp.stack(kp_l).reshape(depth, B, A_HEADS, 2, A_QK_DIM, T), (0, 1, 5, 2, 3, 4))
    v_prompt = jnp.stack(vp_l).reshape(depth, B, T, A_HEADS, A_V_DIM)
    k_sample = jnp.transpose(jnp.stack(ks_l).reshape(depth, A_HEADS, 2, A_QK_DIM, Bs), (0, 4, 1, 2, 3))[:, :, None]
    v_sample = jnp.stack(vs_l).reshape(depth, Bs, 1, A_HEADS, A_V_DIM)
    pool_prompt = jnp.stack(pp_l)
    pool_sample = jnp.transpose(jnp.stack(ps_l), (0, 2, 1, 3))
    return (xp.reshape(B, T, D), xs.reshape(Bs, 1, D), k_prompt, v_prompt, k_sample, v_sample,
            pool_prompt, pool_sample, jnp.stack(sp_l), jnp.stack(ss_l))
```

```python
import functools
import math

import numpy as np
import jax
import jax.numpy as jnp
from jax import lax
from jax.experimental import pallas as pl
from jax.experimental.pallas import tpu as pltpu

F32 = jnp.float32
BF16 = jnp.bfloat16
EPS = 1e-6

A_HEADS = 4
A_QK_DIM = 64
A_V_DIM = 128
HEAD_W = 128
P_WINDOWS = (2, 4, 8, 16)
POOL_BUF = 15
N_BRANCH = 3
BRANCH_W = 512
REL_BUCKETS = 32
REL_MAX_DIST = 128
PAGE = 128
NEG = -1e30
LOG2E = math.log2(math.e)

ZC_Q, ZC_K, ZC_V, ZC_U, ZC_HQ, ZC_HF, ZC_HI, ZC_HG = range(8)
MIX_W = 8 * 512
ZP_Q, ZP_U, ZP_HQ, ZP_HF, ZP_HI, ZP_HG = range(6)

VMEM_LIMIT = 48 * 1024 * 1024
ATT_VMEM_LIMIT = 56 * 1024 * 1024
ATT_TILE = 512
ATT_MAX_SPREAD = 100.0
HGRN_CHUNK = 64
HGRN_SUB = 8
HGRN_HEAD_GROUPS = ((0, 1, 2, 3),)
HGRN_UNROLL = 4


def _cparams(sem):
    return pltpu.CompilerParams(dimension_semantics=sem, vmem_limit_bytes=VMEM_LIMIT)


def _resident(shape):
    nd = len(shape)
    return pl.BlockSpec(shape, lambda *_: (0,) * nd)


def _sigmoid(x):
    return 1.0 / (1.0 + jnp.exp(-x))


def _silu(x):
    return x * _sigmoid(x)


def _normed(x_ref, g_ref):
    x = x_ref[...]
    ms = jnp.mean(x * x, axis=-1, keepdims=True)
    return (x * lax.rsqrt(ms + EPS) * g_ref[...]).astype(BF16)


def _inproj_body(x_ref, g_ref, w_ref, z_ref):
    z_ref[...] = jnp.dot(_normed(x_ref, g_ref), w_ref[...], preferred_element_type=F32)


def _inproj(x, g, w, layer, tm):
    M, D = x.shape
    N = w.shape[2]
    return pl.pallas_call(
        _inproj_body,
        grid=(M // tm,),
        in_specs=[
            pl.BlockSpec((tm, D), lambda i: (i, 0)),
            pl.BlockSpec((1, D), lambda i: (0, 0)),
            pl.BlockSpec((None, D, N), lambda i: (layer, 0, 0), pipeline_mode=pl.Buffered(1)),
        ],
        out_specs=pl.BlockSpec((tm, N), lambda i: (i, 0)),
        out_shape=jax.ShapeDtypeStruct((M, N), F32),
        compiler_params=_cparams(("parallel",)),
        name="inproj",
    )(x, g, w)


def _inproj_prompt_body(x_ref, g_ref, w_ref, gk_ref, zp_ref, kT_ref, kTb_ref, v4_ref, vb1_ref, k_scr,
                        *, tm):
    h = _normed(x_ref, g_ref)

    def cols(c0, c1):
        return jnp.dot(h, w_ref[:, c0 * 512:c1 * 512], preferred_element_type=F32)

    zp_ref[:, 0:512] = cols(ZC_Q, ZC_Q + 1)
    k_scr[...] = cols(ZC_K, ZC_K + 1)
    kn = _group_norm_T(k_scr[...].T, gk_ref[...])
    kT_ref[...] = kn
    kTb_ref[...] = kn.astype(BF16)
    v = cols(ZC_V, ZC_V + 1)
    ones = jnp.ones((tm, HEAD_W), BF16)
    for hd in range(A_HEADS):
        vh = v[:, hd * HEAD_W:(hd + 1) * HEAD_W]
        v4_ref[pl.ds(hd, tm, stride=A_HEADS), :] = vh
        vb1_ref[:, 2 * hd * HEAD_W:(2 * hd + 1) * HEAD_W] = vh.astype(BF16)
        vb1_ref[:, (2 * hd + 1) * HEAD_W:(2 * hd + 2) * HEAD_W] = ones
    for n, c in enumerate(range(ZC_U, ZC_HG + 1)):
        zp_ref[:, (n + 1) * 512:(n + 2) * 512] = cols(c, c + 1)


def _inproj_prompt(x, g, w, gk_col, layer, B, T, tm=512):
    M, D = x.shape
    N = w.shape[2]
    nt = T // tm
    return pl.pallas_call(
        functools.partial(_inproj_prompt_body, tm=tm),
        grid=(B, nt),
        in_specs=[
            pl.BlockSpec((tm, D), lambda b, i: (b * nt + i, 0)),
            pl.BlockSpec((1, D), lambda b, i: (0, 0)),
            pl.BlockSpec((None, D, N), lambda b, i: (layer, 0, 0), pipeline_mode=pl.Buffered(1)),
            pl.BlockSpec((512, 1), lambda b, i: (0, 0)),
        ],
        out_specs=[
            pl.BlockSpec((tm, 6 * 512), lambda b, i: (b * nt + i, 0)),
            pl.BlockSpec((None, 512, tm), lambda b, i: (b, 0, i)),
            pl.BlockSpec((None, 512, tm), lambda b, i: (b, 0, i)),
            pl.BlockSpec((None, A_HEADS * tm, HEAD_W), lambda b, i: (b, i, 0)),
            pl.BlockSpec((tm, 2 * 512), lambda b, i: (b * nt + i, 0)),
        ],
        out_shape=[
            jax.ShapeDtypeStruct((M, 6 * 512), F32),
            jax.ShapeDtypeStruct((B, 512, T), F32),
            jax.ShapeDtypeStruct((B, 512, T), BF16),
            jax.ShapeDtypeStruct((B, A_HEADS * T, HEAD_W), F32),
            jax.ShapeDtypeStruct((M, 2 * 512), BF16),
        ],
        scratch_shapes=[pltpu.VMEM((tm, 512), F32)],
        compiler_params=_cparams(("parallel", "parallel")),
        name="inproj_prompt",
    )(x, g, w, gk_col)


def _group_norm_T(xT, gcol):
    n = xT.shape[1]
    x3 = xT.reshape(8, A_QK_DIM, n)
    ms = jnp.mean(x3 * x3, axis=1, keepdims=True)
    return (x3 * lax.rsqrt(ms + EPS)).reshape(8 * A_QK_DIM, n) * gcol


def _rel_bucket_np(n):
    n = np.asarray(n, np.int32)
    max_exact = REL_BUCKETS // 2
    nf = np.maximum(n, max_exact).astype(np.float32)
    large = max_exact + (np.log(nf / np.float32(max_exact)) / np.float32(math.log(REL_MAX_DIST / max_exact))
                         * np.float32(REL_BUCKETS - max_exact)).astype(np.int32)
    large = np.minimum(large, REL_BUCKETS - 1)
    return np.where(n < max_exact, n, large).astype(np.int32)


def _bias_body(tab_ref, bkt_ref, o_ref, *, scale):
    h = pl.program_id(0)
    bkt = bkt_ref[...]
    far = tab_ref[REL_BUCKETS - 1, h]
    acc = jnp.full(bkt.shape, NEG, F32)
    for b in range(REL_BUCKETS):
        acc = jnp.where(bkt == b, (tab_ref[b, h] - far) * scale, acc)
    o_ref[...] = acc


def _bias_tiles(rel_table, buckets, scale):
    R, C = buckets.shape
    return pl.pallas_call(
        functools.partial(_bias_body, scale=scale),
        grid=(A_HEADS,),
        in_specs=[pl.BlockSpec(memory_space=pltpu.SMEM), _resident((R, C))],
        out_specs=pl.BlockSpec((None, R, C), lambda h: (h, 0, 0)),
        out_shape=jax.ShapeDtypeStruct((A_HEADS, R, C), F32),
        compiler_params=_cparams(("arbitrary",)),
        name="bias_tiles",
    )(rel_table, jnp.asarray(buckets))


def _attn_body(pt_ref, par_ref, q_ref, kT_ref, v1_ref, bias_ref, gq_ref, gs_ref,
               qn_ref, ss_ref, vnew_ref, biasd_ref, *rest, tile, out_scale, npages, spg):
    del pt_ref
    k_refs = rest[:spg * npages]
    v_refs = rest[spg * npages:2 * spg * npages]
    o_ref, od_ref, mb_scr, m_scr, acc_scr = rest[2 * spg * npages:]
    h = pl.program_id(1)
    qi = pl.program_id(2)
    step_id = (pl.program_id(0) * pl.num_programs(1) + h) * pl.num_programs(2) + qi

    def decode():
        for j in range(spg):
            _decode_seq(step_id * spg + j, par_ref[0], qn_ref, ss_ref, vnew_ref, biasd_ref, gs_ref,
                        k_refs[j * npages:(j + 1) * npages], v_refs[j * npages:(j + 1) * npages],
                        od_ref, out_scale)

    q = q_ref[...]
    lane = lax.broadcasted_iota(jnp.int32, q.shape, 1)
    lo = lane < A_QK_DIM
    q2 = q * q
    ms0 = jnp.sum(jnp.where(lo, q2, 0.0), axis=-1, keepdims=True) * (1.0 / A_QK_DIM)
    ms1 = jnp.sum(jnp.where(lo, 0.0, q2), axis=-1, keepdims=True) * (1.0 / A_QK_DIM)
    rs = jnp.where(lo, lax.rsqrt(ms0 + EPS), lax.rsqrt(ms1 + EPS))
    qn = q * rs * gq_ref[...] * (A_QK_DIM ** -0.5 * LOG2E)
    qf = (jnp.where(lo, qn, 0.0), jnp.where(lo, 0.0, qn))
    qs = (qf[0].astype(BF16), qf[1].astype(BF16))
    acc_scr[...] = jnp.zeros(acc_scr.shape, F32)
    diag = pl.multiple_of(qi * tile, tile)
    sub = pl.multiple_of(jnp.maximum(qi - 1, 0) * tile, tile)

    n_far = jnp.maximum(qi - 1, 0)

    @pl.when(par_ref[1] > 0.5)
    def _():
        for mp in range(2):
            nq = jnp.sqrt(jnp.sum(qf[mp] * qf[mp], axis=-1, keepdims=True))
            mb_scr[mp] = jnp.broadcast_to(nq * par_ref[2] + par_ref[3 + h], (tile, tile))

        def pv(ks, bias):
            kt = kT_ref[:, pl.ds(ks, tile)]
            vt = v1_ref[pl.ds(ks, tile), :]
            out = []
            for mp in range(2):
                s = jnp.dot(qs[mp], kt, preferred_element_type=F32)
                if bias is not None:
                    s = s + bias
                p = jnp.exp2(s - mb_scr[mp]).astype(BF16)
                out.append(jnp.dot(p, vt, preferred_element_type=F32))
            return out

        def far_pair(i, carry):
            c0 = pv(pl.multiple_of(2 * i * tile, tile), None)
            c1 = pv(pl.multiple_of((2 * i + 1) * tile, tile), None)
            for mp in range(2):
                acc_scr[mp] = acc_scr[mp] + c0[mp] + c1[mp]
            return carry

        lax.fori_loop(0, n_far >> 1, far_pair, 0)

        @pl.when((n_far & 1) == 1)
        def _():
            c = pv(pl.multiple_of((n_far - 1) * tile, tile), None)
            for mp in range(2):
                acc_scr[mp] += c[mp]

        @pl.when(qi >= 1)
        def _():
            c = pv(sub, bias_ref[:, 0:tile])
            for mp in range(2):
                acc_scr[mp] += c[mp]

        c = pv(diag, bias_ref[:, tile:2 * tile])
        decode()
        for mp in range(2):
            acc_scr[mp] += c[mp]

    @pl.when(par_ref[1] <= 0.5)
    def _():
        m_scr[...] = jnp.full(m_scr.shape, -jnp.inf, F32)

        def step(ks, bias):
            kt = kT_ref[:, pl.ds(ks, tile)]
            vt = v1_ref[pl.ds(ks, tile), :]
            for mp in range(2):
                s = jnp.dot(qs[mp], kt, preferred_element_type=F32)
                if bias is not None:
                    s = s + bias
                m_prev = m_scr[mp]
                m_new = jnp.maximum(m_prev, jnp.max(s, axis=-1, keepdims=True))
                p = jnp.exp2(s - m_new).astype(BF16)
                acc_scr[mp] = (jnp.exp2(m_prev - m_new) * acc_scr[mp]
                               + jnp.dot(p, vt, preferred_element_type=F32))
                m_scr[mp] = m_new

        def far_step(i, carry):
            step(pl.multiple_of(i * tile, tile), None)
            return carry

        lax.fori_loop(0, n_far, far_step, 0)

        @pl.when(qi >= 1)
        def _():
            step(sub, bias_ref[:, 0:tile])

        step(diag, bias_ref[:, tile:2 * tile])
        decode()

    a0 = acc_scr[0]
    a1 = acc_scr[1]
    o = a0[:, :HEAD_W] / a0[:, HEAD_W:] - par_ref[0] * (a1[:, :HEAD_W] / a1[:, HEAD_W:])
    ms = jnp.mean(o * o, axis=-1, keepdims=True)
    o_ref[...] = (o * lax.rsqrt(ms + EPS) * gs_ref[...] * out_scale).astype(o_ref.dtype)


def _attn_prompt(pt_flat, par, z, kTb, vb1, bias, gq2, gs, qn, ss, zs, bias_dec, ckT, cv, layer,
                 B, T, out_scale, npages, spg, tile=ATT_TILE):
    nq = T // tile
    M = B * T
    Bs = qn.shape[1]
    once = pl.Buffered(1)

    def page_spec(j, p):
        def imap(b, h, i, pt):
            seq = ((b * A_HEADS + h) * nq + i) * spg + j
            return (layer, pt[seq * npages + p], 0, 0)
        return pl.BlockSpec((None, None, 512, PAGE), imap)

    pages = [page_spec(j, p) for j in range(spg) for p in range(npages)]
    in_specs = [
        pl.BlockSpec(memory_space=pltpu.SMEM),
        pl.BlockSpec((tile, HEAD_W), lambda b, h, i, pt: (b * nq + i, ZP_Q * 4 + h)),
        pl.BlockSpec((None, HEAD_W, T), lambda b, h, i, pt: (b, h, 0), pipeline_mode=once),
        pl.BlockSpec((T, 2 * HEAD_W), lambda b, h, i, pt: (b, h), pipeline_mode=once),
        pl.BlockSpec((None, tile, 2 * tile), lambda b, h, i, pt: (h, 0, 0), pipeline_mode=once),
        pl.BlockSpec((1, HEAD_W), lambda b, h, i, pt: (0, 0)),
        pl.BlockSpec((1, HEAD_W), lambda b, h, i, pt: (0, 0)),
        pl.BlockSpec((512, Bs), lambda b, h, i, pt: (0, 0)),
        pl.BlockSpec((8, Bs), lambda b, h, i, pt: (0, 0)),
        pl.BlockSpec((Bs, 512), lambda b, h, i, pt: (0, ZC_V)),
        pl.BlockSpec((8, 2 * PAGE), lambda b, h, i, pt: (0, 0)),
    ] + pages + pages
    return pl.pallas_call(
        functools.partial(_attn_body, tile=tile, out_scale=out_scale, npages=npages, spg=spg),
        grid_spec=pltpu.PrefetchScalarGridSpec(
            num_scalar_prefetch=1,
            grid=(B, A_HEADS, nq),
            in_specs=in_specs,
            out_specs=[
                pl.BlockSpec((tile, HEAD_W), lambda b, h, i, pt: (b * nq + i, h)),
                pl.BlockSpec((Bs, 512), lambda b, h, i, pt: (0, 0)),
            ],
            scratch_shapes=[
                pltpu.VMEM((2, tile, tile), F32),
                pltpu.VMEM((2, tile, 1), F32),
                pltpu.VMEM((2, tile, 2 * HEAD_W), F32),
            ],
        ),
        out_shape=[jax.ShapeDtypeStruct((M, A_HEADS * HEAD_W), BF16),
                   jax.ShapeDtypeStruct((Bs, 512), F32)],
        compiler_params=pltpu.CompilerParams(
            dimension_semantics=("arbitrary", "arbitrary", "arbitrary"),
            vmem_limit_bytes=ATT_VMEM_LIMIT),
        name="attn_prompt",
    )(pt_flat, par, z, kTb, vb1, bias, gq2, gs, qn, ss, zs, bias_dec,
      *([ckT] * (spg * npages)), *([cv] * (spg * npages)))


def _pool_mix(ext_ref, base, n, pos, w_ref, sc_ref, o_ref):
    for g, win in enumerate(P_WINDOWS):
        sl = slice(g * HEAD_W, (g + 1) * HEAD_W)
        ug = ext_ref[base:base + n, sl]
        s = ug
        for j in range(1, win):
            s = s + ext_ref[base - j:base - j + n, sl]
        cnt = jnp.minimum(pos + 1, win).astype(F32)
        pooled = s / cnt - ug
        mixed = jnp.dot(pooled.astype(BF16), w_ref[g], preferred_element_type=F32) * sc_ref[:, sl]
        o_ref[:, sl] = mixed.astype(o_ref.dtype)


def _pool_body(u_ref, w_ref, sc_ref, o_ref, ext_scr, *, tm):
    i = pl.program_id(1)
    halo = 16

    @pl.when(i == 0)
    def _():
        ext_scr[0:halo, :] = jnp.zeros((halo, ext_scr.shape[1]), F32)

    @pl.when(i > 0)
    def _():
        ext_scr[0:halo, :] = ext_scr[tm:tm + halo, :]

    ext_scr[halo:halo + tm, :] = u_ref[...]
    pos = i * tm + lax.broadcasted_iota(jnp.int32, (tm, 1), 0)
    _pool_mix(ext_scr, halo, tm, pos, w_ref, sc_ref, o_ref)


def _pool_prompt(z, w_pool, sc, layer, B, T, tm=512):
    nt = T // tm
    return pl.pallas_call(
        functools.partial(_pool_body, tm=tm),
        grid=(B, nt),
        in_specs=[
            pl.BlockSpec((tm, 512), lambda b, i: (b * nt + i, ZP_U)),
            pl.BlockSpec((None, 4, HEAD_W, HEAD_W), lambda b, i: (layer, 0, 0, 0)),
            _resident((1, 512)),
        ],
        out_specs=pl.BlockSpec((tm, 512), lambda b, i: (b * nt + i, 0)),
        out_shape=jax.ShapeDtypeStruct((B * T, 512), BF16),
        scratch_shapes=[pltpu.VMEM((tm + 16, 512), F32)],
        compiler_params=_cparams(("parallel", "arbitrary")),
        name="pool_prompt",
    )(z, w_pool, sc)


def _hgrn_gates(f, log_lb, log1m_lb, one_m_lb):
    ls = jnp.minimum(f, 0.0) - jnp.log(1.0 + jnp.exp(-jnp.abs(f)))
    c = log1m_lb + ls
    mx = jnp.maximum(log_lb, c)
    lg = mx + jnp.log(1.0 + jnp.exp(-jnp.abs(log_lb - c)))
    kk = one_m_lb * _sigmoid(-f)
    return lg, kk


def _hgrn_consts(C):
    r = lax.broadcasted_iota(jnp.int32, (C, C), 0)
    c = lax.broadcasted_iota(jnp.int32, (C, C), 1)
    tri = (r >= c).astype(F32)
    rin = lax.broadcasted_iota(jnp.int32, (C, 1), 0) & (HGRN_SUB - 1)
    validf = [(rin >= d).astype(F32) for d in range(HGRN_SUB)]
    bdiff = (r >> 3) - (c >> 3)
    bandf = [(bdiff == d).astype(F32) for d in range(C // HGRN_SUB)]
    return tri, validf, bandf


def _hgrn_head(q, kk, v, b, ST, consts, C):
    _, validf, bandf = consts
    nb = C // HGRN_SUB
    b3 = b.reshape(nb, HGRN_SUB, HEAD_W)
    kk3 = kk.reshape(nb, HGRN_SUB, HEAD_W)
    v3 = v.reshape(nb, HGRN_SUB, HEAD_W)

    def back(x3, d):
        return pltpu.roll(x3, d, 1).reshape(C, HEAD_W)

    o = jnp.sum(q * kk, axis=-1, keepdims=True) * v
    for d in range(1, HGRN_SUB):
        e = jnp.exp2(jnp.minimum(b - back(b3, d), 0.0))
        a = jnp.sum(q * back(kk3, d) * e, axis=-1, keepdims=True)
        o = o + (a * validf[d]) * back(v3, d)

    bend = b3[:, HGRN_SUB - 1:HGRN_SUB, :]
    bstart = jnp.concatenate([jnp.zeros((1, 1, HEAD_W), F32), bend[:-1]], axis=0)
    qt = (q.reshape(nb, HGRN_SUB, HEAD_W) * jnp.exp2(b3 - bstart)).reshape(C, HEAD_W).astype(BF16)
    a_off = None
    for d in range(1, nb):
        bsh = jnp.concatenate([bend[d - 1:]] + [bend[nb - 1:nb]] * (d - 1), axis=0)
        kd = (kk3 * jnp.exp2(bsh - b3)).reshape(C, HEAD_W).astype(BF16)
        panel = lax.dot_general(qt, kd, (((1,), (1,)), ((), ())), preferred_element_type=F32)
        a_off = panel * bandf[d] if a_off is None else a_off + panel * bandf[d]
    vb = v.astype(BF16)
    o = o + jnp.dot(a_off.astype(BF16), vb, preferred_element_type=F32)

    qe = (q * jnp.exp2(b)).astype(BF16)
    o = o + lax.dot_general(qe, ST.astype(BF16), (((1,), (1,)), ((), ())), preferred_element_type=F32)
    b_end = b[C - 1:C, :]
    ke = (kk * jnp.exp2(b_end - b)).astype(BF16)
    ST_new = ST * jnp.exp2(b_end) + lax.dot_general(vb, ke, (((0,), (0,)), ((), ())),
                                                     preferred_element_type=F32)
    return o, ST_new


def _hgrn_body(q_ref, f_ref, i_ref, g_ref, lbp_ref, gh_ref, o_ref, st_ref, ST_scr, *, tb, C):
    it = pl.program_id(1)

    @pl.when(it == 0)
    def _():
        ST_scr[...] = jnp.zeros(ST_scr.shape, F32)

    lbp = lbp_ref[...]
    gh = gh_ref[...]
    consts = _hgrn_consts(C)

    def chunk_of(heads):
        w = slice(heads[0] * HEAD_W, (heads[-1] + 1) * HEAD_W)

        def chunk(ci, carry):
            rows = pl.ds(pl.multiple_of(ci * C, C), C)
            q = _silu(q_ref[rows, w])
            lg, kk = _hgrn_gates(f_ref[rows, w], lbp[0:1, w], lbp[1:2, w], lbp[2:3, w])
            b = jnp.dot(consts[0], lg * LOG2E, preferred_element_type=F32, precision=lax.Precision.HIGHEST)
            v = i_ref[rows, w]
            gate = _silu(g_ref[rows, w])
            for n, h in enumerate(heads):
                sl = slice(n * HEAD_W, (n + 1) * HEAD_W)
                o, ST_new = _hgrn_head(q[:, sl], kk[:, sl], v[:, sl], b[:, sl], ST_scr[h], consts, C)
                ST_scr[h] = ST_new
                ms = jnp.mean(o * o, axis=-1, keepdims=True)
                o_ref[rows, h * HEAD_W:(h + 1) * HEAD_W] = (
                    o * lax.rsqrt(ms + EPS) * gh * gate[:, sl]).astype(o_ref.dtype)
            return carry

        return chunk

    for heads in HGRN_HEAD_GROUPS:
        lax.fori_loop(0, tb // C, chunk_of(heads), 0, unroll=HGRN_UNROLL)

    @pl.when(it == pl.num_programs(1) - 1)
    def _():
        for h in range(A_HEADS):
            st_ref[h] = ST_scr[h].T


def _hgrn_prompt(z, lbp, gh, B, T, tb=512, C=HGRN_CHUNK):
    nt = T // tb

    def zspec(cb):
        return pl.BlockSpec((tb, 512), lambda b, i: (b * nt + i, cb))

    return pl.pallas_call(
        functools.partial(_hgrn_body, tb=tb, C=C),
        grid=(B, nt),
        in_specs=[zspec(ZP_HQ), zspec(ZP_HF), zspec(ZP_HI), zspec(ZP_HG),
                  _resident((3, 512)), _resident((1, HEAD_W))],
        out_specs=[
            pl.BlockSpec((tb, 512), lambda b, i: (b * nt + i, 0)),
            pl.BlockSpec((None, A_HEADS, HEAD_W, HEAD_W), lambda b, i: (b, 0, 0, 0)),
        ],
        out_shape=[
            jax.ShapeDtypeStruct((B * T, 512), BF16),
            jax.ShapeDtypeStruct((B, A_HEADS, HEAD_W, HEAD_W), F32),
        ],
        scratch_shapes=[pltpu.VMEM((A_HEADS, HEAD_W, HEAD_W), F32)],
        compiler_params=_cparams(("parallel", "arbitrary")),
        name="hgrn_prompt",
    )(z, z, z, z, lbp, gh)


def _merge_body(x_ref, a_ref, p_ref, o_ref, g0_ref, g1_ref, g2_ref, wb_ref, wo_ref, y_ref):
    merged = None
    for n, (br, gz) in enumerate(((a_ref, g0_ref), (p_ref, g1_ref), (o_ref, g2_ref))):
        proj = jnp.dot(br[...].astype(BF16), wb_ref[n], preferred_element_type=F32)
        term = _sigmoid(gz[...]) * proj
        merged = term if merged is None else merged + term
    y_ref[...] = x_ref[...] + jnp.dot(merged.astype(BF16), wo_ref[...], preferred_element_type=F32)


def _merge(x, a, p, o, zg, wb, wo, layer, tm):
    M, D = x.shape

    def gspec(n):
        return pl.BlockSpec((tm, D), lambda i: (i, n))

    bspec = pl.BlockSpec((tm, BRANCH_W), lambda i: (i, 0))
    return pl.pallas_call(
        _merge_body,
        grid=(M // tm,),
        in_specs=[pl.BlockSpec((tm, D), lambda i: (i, 0)), bspec, bspec, bspec,
                  gspec(0), gspec(1), gspec(2),
                  pl.BlockSpec((None, N_BRANCH, BRANCH_W, D), lambda i: (layer, 0, 0, 0)),
                  pl.BlockSpec((None, D, D), lambda i: (layer, 0, 0))],
        out_specs=pl.BlockSpec((tm, D), lambda i: (i, 0)),
        out_shape=jax.ShapeDtypeStruct((M, D), F32),
        compiler_params=_cparams(("parallel",)),
        name="merge",
    )(x, a, p, o, zg, zg, zg, wb, wo)


def _mlp_body(x_ref, g_ref, wu_ref, wd_ref, y_ref, *, fc):
    x = x_ref[...]
    ms = jnp.mean(x * x, axis=-1, keepdims=True)
    hm = (x * lax.rsqrt(ms + EPS) * g_ref[...]).astype(BF16)
    acc = x
    for c in range(wu_ref.shape[1] // fc):
        u = jnp.dot(hm, wu_ref[:, c * fc:(c + 1) * fc], preferred_element_type=F32)
        r = jnp.maximum(u, 0.0)
        acc = acc + jnp.dot((r * r).astype(BF16), wd_ref[c * fc:(c + 1) * fc, :], preferred_element_type=F32)
    y_ref[...] = acc


def _mlp(x, g, wu, wd, layer, tm, fc=1024):
    M, D = x.shape
    Fd = wu.shape[2]
    return pl.pallas_call(
        functools.partial(_mlp_body, fc=fc),
        grid=(M // tm,),
        in_specs=[pl.BlockSpec((tm, D), lambda i: (i, 0)), _resident((1, D)),
                  pl.BlockSpec((None, D, Fd), lambda i: (layer, 0, 0), pipeline_mode=pl.Buffered(1)),
                  pl.BlockSpec((None, Fd, D), lambda i: (layer, 0, 0), pipeline_mode=pl.Buffered(1))],
        out_specs=pl.BlockSpec((tm, D), lambda i: (i, 0)),
        out_shape=jax.ShapeDtypeStruct((M, D), F32),
        compiler_params=_cparams(("parallel",)),
        name="mlp",
    )(x, g, wu, wd)


def _sprep_body(q_ref, k_ref, gq_ref, gk_ref, qn_ref, kT_ref, ss_ref):
    qnT = _group_norm_T(q_ref[...].T, gq_ref[...]) * (A_QK_DIM ** -0.5)
    knT = _group_norm_T(k_ref[...].T, gk_ref[...])
    qn_ref[...] = qnT
    kT_ref[...] = knT
    ss_ref[...] = jnp.sum((qnT * knT).reshape(8, A_QK_DIM, qnT.shape[1]), axis=1)


def _sprep(zs, gq_col, gk_col):
    Bs = zs.shape[0]
    return pl.pallas_call(
        _sprep_body,
        grid=(1,),
        in_specs=[pl.BlockSpec((Bs, 512), lambda i: (0, ZC_Q)),
                  pl.BlockSpec((Bs, 512), lambda i: (0, ZC_K)),
                  _resident((512, 1)), _resident((512, 1))],
        out_specs=[_resident((512, Bs)), _resident((512, Bs)), _resident((8, Bs))],
        out_shape=[jax.ShapeDtypeStruct((512, Bs), F32),
                   jax.ShapeDtypeStruct((512, Bs), F32),
                   jax.ShapeDtypeStruct((8, Bs), F32)],
        compiler_params=_cparams(("arbitrary",)),
        name="sample_prep",
    )(zs, zs, gq_col, gk_col)


def _decode_body(pt_ref, lam_ref, qn_ref, ss_ref, v_ref, bias_ref, gs_ref, *rest, npages, out_scale):
    del pt_ref
    _decode_seq(pl.program_id(0), lam_ref[0], qn_ref, ss_ref, v_ref, bias_ref, gs_ref,
                rest[:npages], rest[npages:2 * npages], rest[2 * npages], out_scale)


def _decode_seq(b, lam, qn_ref, ss_ref, v_ref, bias_ref, gs_ref, k_refs, v_refs, o_ref, out_scale):
    npages = len(k_refs)
    Bs = qn_ref.shape[1]

    lane_q = lax.broadcasted_iota(jnp.int32, qn_ref.shape, 1)
    q_col = jnp.sum(jnp.where(lane_q == b, qn_ref[...], 0.0), axis=-1, keepdims=True)
    q_b = jnp.broadcast_to(q_col, (8 * A_QK_DIM, PAGE))
    s_pages = [jnp.sum((k_refs[p][...] * q_b).reshape(8, A_QK_DIM, PAGE), axis=1)
               for p in range(npages)]
    s_pages[-1] = s_pages[-1] + bias_ref[:, 0:PAGE]
    lane_b = lax.broadcasted_iota(jnp.int32, (8, Bs), 1)
    s_self = (jnp.sum(jnp.where(lane_b == b, ss_ref[...], 0.0), axis=-1, keepdims=True)
              + bias_ref[:, PAGE:PAGE + 1])

    m = s_self
    for s in s_pages:
        m = jnp.maximum(m, jnp.max(s, axis=-1, keepdims=True))
    p_self = jnp.exp(s_self - m)
    p_pages = [jnp.exp(s - m) for s in s_pages]
    l = p_self
    for p in p_pages:
        l = l + jnp.sum(p, axis=-1, keepdims=True)
    row1 = lax.broadcasted_iota(jnp.int32, (8, 1), 0)
    coef = jnp.where((row1 & 1) == 0, 1.0, -lam) / l

    def pair(x):
        return x + pltpu.roll(x, 7, 0)

    w_self = pair(jnp.broadcast_to(p_self * coef, (8, HEAD_W)))
    accs = [jnp.zeros((8, HEAD_W), F32) for _ in range(A_HEADS)]
    for pg in range(npages):
        wp = pair(p_pages[pg] * coef).astype(BF16)
        for h in range(A_HEADS):
            vh = v_refs[pg][pl.ds(h, PAGE, stride=A_HEADS), :].astype(BF16)
            accs[h] = accs[h] + jnp.dot(wp, vh, preferred_element_type=F32)
    v_new = v_ref[pl.ds(b, 1), :]
    outs = []
    for h in range(A_HEADS):
        oh = accs[h][2 * h:2 * h + 1, :] + w_self[2 * h:2 * h + 1, :] * v_new[:, h * HEAD_W:(h + 1) * HEAD_W]
        ms = jnp.mean(oh * oh, axis=-1, keepdims=True)
        outs.append(oh * lax.rsqrt(ms + EPS) * gs_ref[...] * out_scale)
    o_ref[pl.ds(b, 1), :] = jnp.concatenate(outs, axis=1)


def _decode(pt_flat, lam, qn, ss, zs, bias_dec, gs, ckT, cv, layer, npages, out_scale):
    Bs = qn.shape[1]

    def page_spec(p):
        return pl.BlockSpec((None, None, 512, PAGE),
                            lambda b, pt, p=p: (layer, pt[b * npages + p], 0, 0))

    in_specs = [
        pl.BlockSpec(memory_space=pltpu.SMEM),
        pl.BlockSpec((512, Bs), lambda b, pt: (0, 0)),
        pl.BlockSpec((8, Bs), lambda b, pt: (0, 0)),
        pl.BlockSpec((Bs, 512), lambda b, pt: (0, ZC_V)),
        pl.BlockSpec((8, 2 * PAGE), lambda b, pt: (0, 0)),
        pl.BlockSpec((1, HEAD_W), lambda b, pt: (0, 0)),
    ] + [page_spec(p) for p in range(npages)] * 2
    return pl.pallas_call(
        functools.partial(_decode_body, npages=npages, out_scale=out_scale),
        grid_spec=pltpu.PrefetchScalarGridSpec(
            num_scalar_prefetch=1,
            grid=(Bs,),
            in_specs=in_specs,
            out_specs=pl.BlockSpec((Bs, 512), lambda b, pt: (0, 0)),
        ),
        out_shape=jax.ShapeDtypeStruct((Bs, 512), F32),
        compiler_params=_cparams(("arbitrary",)),
        name="decode_attn",
    )(pt_flat, lam, qn, ss, zs, bias_dec, gs, *([ckT] * npages), *([cv] * npages))


def _smix_body(q_ref, f_ref, i_ref, g_ref, u_ref, sp_ref, s0_ref, lbp_ref, gh_ref, wp_ref, sc_ref,
               o_ref, p_ref, spn_ref, s1_ref, o_scr, *, ns, past_len):
    q = _silu(q_ref[...])
    lg, kk = _hgrn_gates(f_ref[...], lbp_ref[0:1, :], lbp_ref[1:2, :], lbp_ref[2:3, :])
    g = jnp.exp(lg)
    v = i_ref[...]
    eye = (lax.broadcasted_iota(jnp.int32, (HEAD_W, HEAD_W), 0)
           == lax.broadcasted_iota(jnp.int32, (HEAD_W, HEAD_W), 1))

    def col_of(row):
        return jnp.sum(jnp.where(eye, jnp.broadcast_to(row, eye.shape), 0.0), axis=-1, keepdims=True)

    for s in range(ns):
        for h in range(A_HEADS):
            sl = slice(h * HEAD_W, (h + 1) * HEAD_W)
            S_new = (col_of(g[s:s + 1, sl]) * s0_ref[s, h]
                     + col_of(kk[s:s + 1, sl]) * v[s:s + 1, sl])
            s1_ref[s, h] = S_new
            o_scr[s:s + 1, sl] = jnp.sum(col_of(q[s:s + 1, sl]) * S_new, axis=0, keepdims=True)
    zg = g_ref[...]
    for h in range(A_HEADS):
        sl = slice(h * HEAD_W, (h + 1) * HEAD_W)
        oh = o_scr[:, sl]
        ms = jnp.mean(oh * oh, axis=-1, keepdims=True)
        o_ref[:, sl] = oh * lax.rsqrt(ms + EPS) * gh_ref[...] * _silu(zg[:, sl])

    u = u_ref[...]
    for g_i, win in enumerate(P_WINDOWS):
        sl = slice(g_i * HEAD_W, (g_i + 1) * HEAD_W)
        ug = u[:, sl]
        sacc = ug
        for j in range(1, win):
            sacc = sacc + sp_ref[POOL_BUF - j][:, sl]
        cnt = float(min(past_len + 1, win))
        pooled = sacc / cnt - ug
        p_ref[:, sl] = (jnp.dot(pooled.astype(BF16), wp_ref[g_i], preferred_element_type=F32)
                        * sc_ref[:, sl])
    for j in range(POOL_BUF - 1):
        spn_ref[j] = sp_ref[j + 1]
    spn_ref[POOL_BUF - 1] = u


def _smix(zs, sp_t, s0, lbp, gh, w_pool, sc, layer, past_len, ns=8):
    Bs = zs.shape[0]

    def zspec(cb):
        return pl.BlockSpec((ns, 512), lambda i: (i, cb))

    return pl.pallas_call(
        functools.partial(_smix_body, ns=ns, past_len=past_len),
        grid=(Bs // ns,),
        in_specs=[zspec(ZC_HQ), zspec(ZC_HF), zspec(ZC_HI), zspec(ZC_HG), zspec(ZC_U),
                  pl.BlockSpec((None, POOL_BUF, ns, 512), lambda i: (layer, 0, i, 0)),
                  pl.BlockSpec((None, ns, A_HEADS, HEAD_W, HEAD_W), lambda i: (layer, i, 0, 0, 0)),
                  _resident((3, 512)), _resident((1, HEAD_W)),
                  pl.BlockSpec((None, 4, HEAD_W, HEAD_W), lambda i: (layer, 0, 0, 0)),
                  _resident((1, 512))],
        out_specs=[pl.BlockSpec((ns, 512), lambda i: (i, 0)),
                   pl.BlockSpec((ns, 512), lambda i: (i, 0)),
                   pl.BlockSpec((POOL_BUF, ns, 512), lambda i: (0, i, 0)),
                   pl.BlockSpec((ns, A_HEADS, HEAD_W, HEAD_W), lambda i: (i, 0, 0, 0))],
        out_shape=[jax.ShapeDtypeStruct((Bs, 512), F32),
                   jax.ShapeDtypeStruct((Bs, 512), F32),
                   jax.ShapeDtypeStruct((POOL_BUF, Bs, 512), F32),
                   jax.ShapeDtypeStruct((Bs, A_HEADS, HEAD_W, HEAD_W), F32)],
        scratch_shapes=[pltpu.VMEM((ns, 512), F32)],
        compiler_params=_cparams(("parallel",)),
        name="sample_mix",
    )(zs, zs, zs, zs, zs, sp_t, s0, lbp, gh, w_pool, sc)


def kernel(x_prompt, x_sample, cache_k, cache_v, state_pool, state_hgrn, page_table, rel_table, lb_param, w_in, g_mix, g_q, g_k, lam_p, g_sub, w_pool, pool_scale, g_h, w_branch, w_out, g_mlp, w_up, w_down):
    B, T, D = x_prompt.shape
    Bs = x_sample.shape[0]
    depth, n_phys = cache_k.shape[:2]
    npages = page_table.shape[1]
    past_len = npages * PAGE
    tile = ATT_TILE
    att_steps = B * A_HEADS * (T // tile)
    spg = Bs // att_steps if Bs % att_steps == 0 else 0

    lb_all = jnp.cumsum(jax.nn.softmax(lb_param.astype(F32), axis=0), axis=0)
    lb_all = lb_all - lb_all[:1]
    lbp_all = jnp.stack([jnp.log(lb_all), jnp.log1p(-lb_all), 1.0 - lb_all], axis=1)
    lp = lam_p.astype(F32)
    lam_dyn = jnp.exp(jnp.sum(lp[:, 0] * lp[:, 1], axis=-1)) - jnp.exp(jnp.sum(lp[:, 2] * lp[:, 3], axis=-1))
    w_mix_b = w_in[:, :, :MIX_W].astype(BF16)
    w_gate_b = w_in[:, :, MIX_W:].astype(BF16)
    w_pool_b, w_branch_b = w_pool.astype(BF16), w_branch.astype(BF16)
    w_out_b, w_up_b, w_down_b = w_out.astype(BF16), w_up.astype(BF16), w_down.astype(BF16)

    rr = np.arange(tile)[:, None]
    cc = np.arange(2 * tile)[None, :]
    dist = rr + tile - cc
    bkt_prompt = np.where(dist >= 0, _rel_bucket_np(np.maximum(dist, 0)), -1).astype(np.int32)
    bias_prompt = _bias_tiles(rel_table, bkt_prompt, LOG2E)
    dd = np.concatenate([PAGE - np.arange(PAGE), np.zeros(PAGE, np.int64)])
    bkt_dec = np.broadcast_to(_rel_bucket_np(dd)[None, :], (8, 2 * PAGE)).astype(np.int32)
    bias_dec_h = _bias_tiles(rel_table, bkt_dec, 1.0)
    bias_dec = jnp.repeat(bias_dec_h[:, 0, :], 2, axis=0)

    tab2 = (rel_table.astype(F32) - rel_table[REL_BUCKETS - 1:].astype(F32)) * LOG2E
    bias_hi = jnp.maximum(jnp.max(tab2, axis=0), 0.0)
    bias_lo = jnp.minimum(jnp.min(tab2, axis=0), 0.0)
    slack = 1.01
    k_bound = 8.0 * slack * jnp.max(jnp.abs(g_k.astype(F32)), axis=-1)
    q_bound = 8.0 * slack * (A_QK_DIM ** -0.5 * LOG2E) * jnp.max(jnp.abs(g_q.astype(F32)), axis=-1)
    spread = 2.0 * q_bound * k_bound + jnp.max(bias_hi - bias_lo)
    bounded = (spread <= ATT_MAX_SPREAD).astype(F32)

    ckT = jnp.transpose(cache_k, (0, 1, 3, 4, 5, 2)).reshape(depth, n_phys, 512, PAGE)
    cv4 = cache_v.reshape(depth, n_phys, A_HEADS * PAGE, HEAD_W)
    sp_t = jnp.transpose(state_pool, (0, 2, 1, 3))
    pt_flat = page_table.reshape(-1).astype(jnp.int32)

    xp = x_prompt.reshape(B * T, D)
    xs = x_sample.reshape(Bs, D)
    kp_l, vp_l, ks_l, vs_l, pp_l, ps_l, sp_l, ss_l = [], [], [], [], [], [], [], []
    for l in range(depth):
        lam_init = 0.8 - 0.6 * math.exp(-0.3 * l)
        out_scale = 1.0 - lam_init
        lam = (lam_dyn[l] + lam_init).reshape(1).astype(F32)
        par = jnp.concatenate([lam, bounded[l].reshape(1), k_bound[l].reshape(1), bias_hi]).astype(F32)
        gmix = g_mix[l].reshape(1, D)
        gq_col = jnp.tile(g_q[l], 8).reshape(512, 1)
        gk_col = jnp.tile(g_k[l], 8).reshape(512, 1)
        gq2 = jnp.tile(g_q[l], 2).reshape(1, HEAD_W)
        gs = g_sub[l].reshape(1, HEAD_W)
        gh = g_h[l].reshape(1, HEAD_W)
        sc = pool_scale[l].reshape(1, 512)
        gmlp = g_mlp[l].reshape(1, D)
        lbp = lbp_all[l]

        zs = _inproj(xs, gmix, w_mix_b, l, tm=Bs)
        zgs = _inproj(xs, gmix, w_gate_b, l, tm=Bs)
        qn, kTs, ssf = _sprep(zs, gq_col, gk_col)
        z, kT, kTb, v4, vb1 = _inproj_prompt(xp, gmix, w_mix_b, gk_col, l, B, T)
        zg = _inproj(xp, gmix, w_gate_b, l, tm=512)

        a, a_s = _attn_prompt(pt_flat, par, z, kTb, vb1, bias_prompt, gq2, gs, qn, ssf, zs, bias_dec,
                              ckT, cv4, l, B, T, out_scale, npages, spg)
        if spg == 0:
            a_s = _decode(pt_flat, lam, qn, ssf, zs, bias_dec, gs, ckT, cv4, l, npages, out_scale)

        p = _pool_prompt(z, w_pool_b, sc, l, B, T)
        o, st = _hgrn_prompt(z, lbp, gh, B, T)
        x1 = _merge(xp, a, p, o, zg, w_branch_b, w_out_b, l, tm=512)
        xp = _mlp(x1, gmlp, w_up_b, w_down_b, l, tm=512)
        kp_l.append(kT)
        vp_l.append(v4)
        pp_l.append(z.reshape(B, T, -1)[:, T - POOL_BUF:, ZP_U * 512:(ZP_U + 1) * 512])
        sp_l.append(st)

        o_s, p_s, spn, s1 = _smix(zs, sp_t, state_hgrn, lbp, gh, w_pool_b, sc, l, past_len)
        x1s = _merge(xs, a_s, p_s, o_s, zgs, w_branch_b, w_out_b, l, tm=Bs)
        xs = _mlp(x1s, gmlp, w_up_b, w_down_b, l, tm=Bs)
        ks_l.append(kTs)
        vs_l.append(zs[:, ZC_V * 512:(ZC_V + 1) * 512])
        ps_l.append(spn)
        ss_l.append(s1)

    k_prompt = jnp.transpose(jnp.stack(kp_l).reshape(depth, B, A_HEADS, 2, A_QK_DIM, T), (0, 1, 5, 2, 3, 4))
    v_prompt = jnp.stack(vp_l).reshape(depth, B, T, A_HEADS, A_V_DIM)
    k_sample = jnp.transpose(jnp.stack(ks_l).reshape(depth, A_HEADS, 2, A_QK_DIM, Bs), (0, 4, 1, 2, 3))[:, :, None]
    v_sample = jnp.stack(vs_l).reshape(depth, Bs, 1, A_HEADS, A_V_DIM)
    pool_prompt = jnp.stack(pp_l)
    pool_sample = jnp.transpose(jnp.stack(ps_l), (0, 2, 1, 3))
    return (xp.reshape(B, T, D), xs.reshape(Bs, 1, D), k_prompt, v_prompt, k_sample, v_sample,
            pool_prompt, pool_sample, jnp.stack(sp_l), jnp.stack(ss_l))
```

```python
import functools
import math

import numpy as np
import jax
import jax.numpy as jnp
from jax import lax
from jax.experimental import pallas as pl
from jax.experimental.pallas import tpu as pltpu

F32 = jnp.float32
BF16 = jnp.bfloat16
EPS = 1e-6

A_HEADS = 4
A_QK_DIM = 64
A_V_DIM = 128
HEAD_W = 128
P_WINDOWS = (2, 4, 8, 16)
POOL_BUF = 15
N_BRANCH = 3
BRANCH_W = 512
REL_BUCKETS = 32
REL_MAX_DIST = 128
PAGE = 128
NEG = -1e30
LOG2E = math.log2(math.e)

ZC_Q, ZC_K, ZC_V, ZC_U, ZC_HQ, ZC_HF, ZC_HI, ZC_HG = range(8)
MIX_W = 8 * 512
ZP_Q, ZP_U, ZP_HQ, ZP_HF, ZP_HI, ZP_HG = range(6)

VMEM_LIMIT = 48 * 1024 * 1024
ATT_VMEM_LIMIT = 56 * 1024 * 1024
ATT_TILE = 512
ATT_MAX_SPREAD = 100.0
HGRN_CHUNK = 64
HGRN_SUB = 8
HGRN_HEAD_GROUPS = ((0, 1, 2, 3),)
HGRN_UNROLL = 4


def _cparams(sem):
    return pltpu.CompilerParams(dimension_semantics=sem, vmem_limit_bytes=VMEM_LIMIT)


def _resident(shape):
    nd = len(shape)
    return pl.BlockSpec(shape, lambda *_: (0,) * nd)


def _sigmoid(x):
    return 0.5 * jnp.tanh(0.5 * x) + 0.5


def _silu(x):
    return x * _sigmoid(x)


def _normed(x_ref, g_ref):
    x = x_ref[...]
    ms = jnp.mean(x * x, axis=-1, keepdims=True)
    return (x * lax.rsqrt(ms + EPS) * g_ref[...]).astype(BF16)


def _inproj_body(x_ref, g_ref, w_ref, z_ref):
    z_ref[...] = jnp.dot(_normed(x_ref, g_ref), w_ref[...], preferred_element_type=F32)


def _inproj(x, g, w, layer, tm):
    M, D = x.shape
    N = w.shape[2]
    return pl.pallas_call(
        _inproj_body,
        grid=(M // tm,),
        in_specs=[
            pl.BlockSpec((tm, D), lambda i: (i, 0)),
            pl.BlockSpec((1, D), lambda i: (0, 0)),
            pl.BlockSpec((None, D, N), lambda i: (layer, 0, 0), pipeline_mode=pl.Buffered(1)),
        ],
        out_specs=pl.BlockSpec((tm, N), lambda i: (i, 0)),
        out_shape=jax.ShapeDtypeStruct((M, N), F32),
        compiler_params=_cparams(("parallel",)),
        name="inproj",
    )(x, g, w)


def _inproj_prompt_body(x_ref, g_ref, w_ref, gk_ref, kT_in, v4_in, zp_ref, kT_ref, kTb_ref, v4_ref,
                        vb1_ref, k_scr, *, tm):
    del kT_in, v4_in
    h = _normed(x_ref, g_ref)

    def cols(c0, c1):
        return jnp.dot(h, w_ref[:, c0 * 512:c1 * 512], preferred_element_type=F32)

    zp_ref[:, 0:512] = cols(ZC_Q, ZC_Q + 1)
    k_scr[...] = cols(ZC_K, ZC_K + 1)
    kn = _group_norm_T(k_scr[...].T, gk_ref[...])
    kT_ref[...] = kn
    kTb_ref[...] = kn.astype(BF16)
    v = cols(ZC_V, ZC_V + 1)
    ones = jnp.ones((tm, HEAD_W), BF16)
    for hd in range(A_HEADS):
        vh = v[:, hd * HEAD_W:(hd + 1) * HEAD_W]
        v4_ref[pl.ds(hd, tm, stride=A_HEADS), :] = vh
        vb1_ref[:, 2 * hd * HEAD_W:(2 * hd + 1) * HEAD_W] = vh.astype(BF16)
        vb1_ref[:, (2 * hd + 1) * HEAD_W:(2 * hd + 2) * HEAD_W] = ones
    for n, c in enumerate(range(ZC_U, ZC_HG + 1)):
        zp_ref[:, (n + 1) * 512:(n + 2) * 512] = cols(c, c + 1)


def _inproj_prompt(x, g, w, gk_col, kT_all, v4_all, layer, B, T, tm=512):
    M, D = x.shape
    N = w.shape[2]
    nt = T // tm
    return pl.pallas_call(
        functools.partial(_inproj_prompt_body, tm=tm),
        grid=(B, nt),
        in_specs=[
            pl.BlockSpec((tm, D), lambda b, i: (b * nt + i, 0)),
            pl.BlockSpec((1, D), lambda b, i: (0, 0)),
            pl.BlockSpec((None, D, N), lambda b, i: (layer, 0, 0), pipeline_mode=pl.Buffered(1)),
            pl.BlockSpec((512, 1), lambda b, i: (0, 0)),
            pl.BlockSpec(memory_space=pl.ANY),
            pl.BlockSpec(memory_space=pl.ANY),
        ],
        out_specs=[
            pl.BlockSpec((tm, 6 * 512), lambda b, i: (b * nt + i, 0)),
            pl.BlockSpec((None, None, 512, tm), lambda b, i: (layer, b, 0, i)),
            pl.BlockSpec((None, 512, tm), lambda b, i: (b, 0, i)),
            pl.BlockSpec((None, None, A_HEADS * tm, HEAD_W), lambda b, i: (layer, b, i, 0)),
            pl.BlockSpec((tm, 2 * 512), lambda b, i: (b * nt + i, 0)),
        ],
        out_shape=[
            jax.ShapeDtypeStruct((M, 6 * 512), F32),
            jax.ShapeDtypeStruct(kT_all.shape, F32),
            jax.ShapeDtypeStruct((B, 512, T), BF16),
            jax.ShapeDtypeStruct(v4_all.shape, F32),
            jax.ShapeDtypeStruct((M, 2 * 512), BF16),
        ],
        input_output_aliases={4: 1, 5: 3},
        scratch_shapes=[pltpu.VMEM((tm, 512), F32)],
        compiler_params=_cparams(("parallel", "parallel")),
        name="inproj_prompt",
    )(x, g, w, gk_col, kT_all, v4_all)


def _group_norm_T(xT, gcol):
    n = xT.shape[1]
    x3 = xT.reshape(8, A_QK_DIM, n)
    ms = jnp.mean(x3 * x3, axis=1, keepdims=True)
    return (x3 * lax.rsqrt(ms + EPS)).reshape(8 * A_QK_DIM, n) * gcol


def _rel_bucket_np(n):
    n = np.asarray(n, np.int32)
    max_exact = REL_BUCKETS // 2
    nf = np.maximum(n, max_exact).astype(np.float32)
    large = max_exact + (np.log(nf / np.float32(max_exact)) / np.float32(math.log(REL_MAX_DIST / max_exact))
                         * np.float32(REL_BUCKETS - max_exact)).astype(np.int32)
    large = np.minimum(large, REL_BUCKETS - 1)
    return np.where(n < max_exact, n, large).astype(np.int32)


def _bias_body(tab_ref, bkt_ref, o_ref, *, scale):
    h = pl.program_id(0)
    bkt = bkt_ref[...]
    far = tab_ref[REL_BUCKETS - 1, h]
    acc = jnp.full(bkt.shape, NEG, F32)
    for b in range(REL_BUCKETS):
        acc = jnp.where(bkt == b, (tab_ref[b, h] - far) * scale, acc)
    o_ref[...] = acc


def _bias_tiles(rel_table, buckets, scale):
    R, C = buckets.shape
    return pl.pallas_call(
        functools.partial(_bias_body, scale=scale),
        grid=(A_HEADS,),
        in_specs=[pl.BlockSpec(memory_space=pltpu.SMEM), _resident((R, C))],
        out_specs=pl.BlockSpec((None, R, C), lambda h: (h, 0, 0)),
        out_shape=jax.ShapeDtypeStruct((A_HEADS, R, C), F32),
        compiler_params=_cparams(("arbitrary",)),
        name="bias_tiles",
    )(rel_table, jnp.asarray(buckets))


def _attn_body(pt_ref, par_ref, q_ref, kT_ref, v1_ref, bias_ref, gq_ref, gs_ref,
               qn_ref, ss_ref, vnew_ref, biasd_ref, *rest, tile, out_scale, npages, spg):
    del pt_ref
    k_refs = rest[:spg * npages]
    v_refs = rest[spg * npages:2 * spg * npages]
    o_ref, od_ref, mb_scr, m_scr, acc_scr = rest[2 * spg * npages:]
    h = pl.program_id(1)
    qi = pl.program_id(2)
    step_id = (pl.program_id(0) * pl.num_programs(1) + h) * pl.num_programs(2) + qi

    def decode():
        for j in range(spg):
            _decode_seq(step_id * spg + j, par_ref[0], qn_ref, ss_ref, vnew_ref, biasd_ref, gs_ref,
                        k_refs[j * npages:(j + 1) * npages], v_refs[j * npages:(j + 1) * npages],
                        od_ref, out_scale)

    q = q_ref[...]
    lane = lax.broadcasted_iota(jnp.int32, q.shape, 1)
    lo = lane < A_QK_DIM
    q2 = q * q
    ms0 = jnp.sum(jnp.where(lo, q2, 0.0), axis=-1, keepdims=True) * (1.0 / A_QK_DIM)
    ms1 = jnp.sum(jnp.where(lo, 0.0, q2), axis=-1, keepdims=True) * (1.0 / A_QK_DIM)
    rs = jnp.where(lo, lax.rsqrt(ms0 + EPS), lax.rsqrt(ms1 + EPS))
    qn = q * rs * gq_ref[...] * (A_QK_DIM ** -0.5 * LOG2E)
    qf = (jnp.where(lo, qn, 0.0), jnp.where(lo, 0.0, qn))
    qs = (qf[0].astype(BF16), qf[1].astype(BF16))
    acc_scr[...] = jnp.zeros(acc_scr.shape, F32)
    diag = pl.multiple_of(qi * tile, tile)
    sub = pl.multiple_of(jnp.maximum(qi - 1, 0) * tile, tile)

    n_far = jnp.maximum(qi - 1, 0)

    @pl.when(par_ref[1] > 0.5)
    def _():
        for mp in range(2):
            nq = jnp.sqrt(jnp.sum(qf[mp] * qf[mp], axis=-1, keepdims=True))
            mb_scr[mp] = jnp.broadcast_to(nq * par_ref[2] + par_ref[3 + h], (tile, tile))

        def pv(ks, bias):
            kt = kT_ref[:, pl.ds(ks, tile)]
            vt = v1_ref[pl.ds(ks, tile), :]
            out = []
            for mp in range(2):
                s = jnp.dot(qs[mp], kt, preferred_element_type=F32)
                if bias is not None:
                    s = s + bias
                p = jnp.exp2(s - mb_scr[mp]).astype(BF16)
                out.append(jnp.dot(p, vt, preferred_element_type=F32))
            return out

        def far_pair(i, carry):
            c0 = pv(pl.multiple_of(2 * i * tile, tile), None)
            c1 = pv(pl.multiple_of((2 * i + 1) * tile, tile), None)
            for mp in range(2):
                acc_scr[mp] = acc_scr[mp] + c0[mp] + c1[mp]
            return carry

        lax.fori_loop(0, n_far >> 1, far_pair, 0)

        @pl.when((n_far & 1) == 1)
        def _():
            c = pv(pl.multiple_of((n_far - 1) * tile, tile), None)
            for mp in range(2):
                acc_scr[mp] += c[mp]

        @pl.when(qi >= 1)
        def _():
            c = pv(sub, bias_ref[:, 0:tile])
            for mp in range(2):
                acc_scr[mp] += c[mp]

        c = pv(diag, bias_ref[:, tile:2 * tile])
        decode()
        for mp in range(2):
            acc_scr[mp] += c[mp]

    @pl.when(par_ref[1] <= 0.5)
    def _():
        m_scr[...] = jnp.full(m_scr.shape, -jnp.inf, F32)

        def step(ks, bias):
            kt = kT_ref[:, pl.ds(ks, tile)]
            vt = v1_ref[pl.ds(ks, tile), :]
            for mp in range(2):
                s = jnp.dot(qs[mp], kt, preferred_element_type=F32)
                if bias is not None:
                    s = s + bias
                m_prev = m_scr[mp]
                m_new = jnp.maximum(m_prev, jnp.max(s, axis=-1, keepdims=True))
                p = jnp.exp2(s - m_new).astype(BF16)
                acc_scr[mp] = (jnp.exp2(m_prev - m_new) * acc_scr[mp]
                               + jnp.dot(p, vt, preferred_element_type=F32))
                m_scr[mp] = m_new

        def far_step(i, carry):
            step(pl.multiple_of(i * tile, tile), None)
            return carry

        lax.fori_loop(0, n_far, far_step, 0)

        @pl.when(qi >= 1)
        def _():
            step(sub, bias_ref[:, 0:tile])

        step(diag, bias_ref[:, tile:2 * tile])
        decode()

    a0 = acc_scr[0]
    a1 = acc_scr[1]
    o = a0[:, :HEAD_W] / a0[:, HEAD_W:] - par_ref[0] * (a1[:, :HEAD_W] / a1[:, HEAD_W:])
    ms = jnp.mean(o * o, axis=-1, keepdims=True)
    o_ref[...] = (o * lax.rsqrt(ms + EPS) * gs_ref[...] * out_scale).astype(o_ref.dtype)


def _attn_prompt(pt_flat, par, z, kTb, vb1, bias, gq2, gs, qn, ss, zs, bias_dec, ckT, cv, layer,
                 B, T, out_scale, npages, spg, tile=ATT_TILE):
    nq = T // tile
    M = B * T
    Bs = qn.shape[1]
    once = pl.Buffered(1)

    def page_spec(j, p):
        def imap(b, h, i, pt):
            seq = ((b * A_HEADS + h) * nq + i) * spg + j
            return (layer, pt[seq * npages + p], 0, 0)
        return pl.BlockSpec((None, None, 512, PAGE), imap)

    pages = [page_spec(j, p) for j in range(spg) for p in range(npages)]
    in_specs = [
        pl.BlockSpec(memory_space=pltpu.SMEM),
        pl.BlockSpec((tile, HEAD_W), lambda b, h, i, pt: (b * nq + i, ZP_Q * 4 + h)),
        pl.BlockSpec((None, HEAD_W, T), lambda b, h, i, pt: (b, h, 0), pipeline_mode=once),
        pl.BlockSpec((T, 2 * HEAD_W), lambda b, h, i, pt: (b, h), pipeline_mode=once),
        pl.BlockSpec((None, tile, 2 * tile), lambda b, h, i, pt: (h, 0, 0), pipeline_mode=once),
        pl.BlockSpec((1, HEAD_W), lambda b, h, i, pt: (0, 0)),
        pl.BlockSpec((1, HEAD_W), lambda b, h, i, pt: (0, 0)),
        pl.BlockSpec((512, Bs), lambda b, h, i, pt: (0, 0)),
        pl.BlockSpec((8, Bs), lambda b, h, i, pt: (0, 0)),
        pl.BlockSpec((Bs, 512), lambda b, h, i, pt: (0, ZC_V)),
        pl.BlockSpec((8, 2 * PAGE), lambda b, h, i, pt: (0, 0)),
    ] + pages + pages
    return pl.pallas_call(
        functools.partial(_attn_body, tile=tile, out_scale=out_scale, npages=npages, spg=spg),
        grid_spec=pltpu.PrefetchScalarGridSpec(
            num_scalar_prefetch=1,
            grid=(B, A_HEADS, nq),
            in_specs=in_specs,
            out_specs=[
                pl.BlockSpec((tile, HEAD_W), lambda b, h, i, pt: (b * nq + i, h)),
                pl.BlockSpec((Bs, 512), lambda b, h, i, pt: (0, 0)),
            ],
            scratch_shapes=[
                pltpu.VMEM((2, tile, tile), F32),
                pltpu.VMEM((2, tile, 1), F32),
                pltpu.VMEM((2, tile, 2 * HEAD_W), F32),
            ],
        ),
        out_shape=[jax.ShapeDtypeStruct((M, A_HEADS * HEAD_W), BF16),
                   jax.ShapeDtypeStruct((Bs, 512), F32)],
        compiler_params=pltpu.CompilerParams(
            dimension_semantics=("arbitrary", "arbitrary", "arbitrary"),
            vmem_limit_bytes=ATT_VMEM_LIMIT),
        name="attn_prompt",
    )(pt_flat, par, z, kTb, vb1, bias, gq2, gs, qn, ss, zs, bias_dec,
      *([ckT] * (spg * npages)), *([cv] * (spg * npages)))


def _pool_mix(ext_ref, base, n, pos, w_ref, sc_ref, o_ref):
    for g, win in enumerate(P_WINDOWS):
        sl = slice(g * HEAD_W, (g + 1) * HEAD_W)
        ug = ext_ref[base:base + n, sl]
        s = ug
        for j in range(1, win):
            s = s + ext_ref[base - j:base - j + n, sl]
        cnt = jnp.minimum(pos + 1, win).astype(F32)
        pooled = s / cnt - ug
        mixed = jnp.dot(pooled.astype(BF16), w_ref[g], preferred_element_type=F32) * sc_ref[:, sl]
        o_ref[:, sl] = mixed.astype(o_ref.dtype)


def _pool_body(u_ref, w_ref, sc_ref, o_ref, ext_scr, *, tm):
    i = pl.program_id(1)
    halo = 16

    @pl.when(i == 0)
    def _():
        ext_scr[0:halo, :] = jnp.zeros((halo, ext_scr.shape[1]), F32)

    @pl.when(i > 0)
    def _():
        ext_scr[0:halo, :] = ext_scr[tm:tm + halo, :]

    ext_scr[halo:halo + tm, :] = u_ref[...]
    pos = i * tm + lax.broadcasted_iota(jnp.int32, (tm, 1), 0)
    _pool_mix(ext_scr, halo, tm, pos, w_ref, sc_ref, o_ref)


def _pool_prompt(z, w_pool, sc, layer, B, T, tm=512):
    nt = T // tm
    return pl.pallas_call(
        functools.partial(_pool_body, tm=tm),
        grid=(B, nt),
        in_specs=[
            pl.BlockSpec((tm, 512), lambda b, i: (b * nt + i, ZP_U)),
            pl.BlockSpec((None, 4, HEAD_W, HEAD_W), lambda b, i: (layer, 0, 0, 0)),
            _resident((1, 512)),
        ],
        out_specs=pl.BlockSpec((tm, 512), lambda b, i: (b * nt + i, 0)),
        out_shape=jax.ShapeDtypeStruct((B * T, 512), BF16),
        scratch_shapes=[pltpu.VMEM((tm + 16, 512), F32)],
        compiler_params=_cparams(("parallel", "arbitrary")),
        name="pool_prompt",
    )(z, w_pool, sc)


def _hgrn_gates(f, log_lb, log1m_lb, one_m_lb):
    ls = jnp.minimum(f, 0.0) - jnp.log(1.0 + jnp.exp(-jnp.abs(f)))
    c = log1m_lb + ls
    mx = jnp.maximum(log_lb, c)
    lg = mx + jnp.log(1.0 + jnp.exp(-jnp.abs(log_lb - c)))
    kk = one_m_lb * _sigmoid(-f)
    return lg, kk


def _hgrn_consts(C):
    r = lax.broadcasted_iota(jnp.int32, (C, C), 0)
    c = lax.broadcasted_iota(jnp.int32, (C, C), 1)
    tri = (r >= c).astype(F32)
    rin = lax.broadcasted_iota(jnp.int32, (C, 1), 0) & (HGRN_SUB - 1)
    validf = [(rin >= d).astype(F32) for d in range(HGRN_SUB)]
    bdiff = (r >> 3) - (c >> 3)
    bandf = [(bdiff == d).astype(F32) for d in range(C // HGRN_SUB)]
    return tri, validf, bandf


def _hgrn_head(q, kk, v, b, ST, consts, C):
    _, validf, bandf = consts
    nb = C // HGRN_SUB
    b3 = b.reshape(nb, HGRN_SUB, HEAD_W)
    kk3 = kk.reshape(nb, HGRN_SUB, HEAD_W)
    v3 = v.reshape(nb, HGRN_SUB, HEAD_W)

    def back(x3, d):
        return pltpu.roll(x3, d, 1).reshape(C, HEAD_W)

    o = jnp.sum(q * kk, axis=-1, keepdims=True) * v
    for d in range(1, HGRN_SUB):
        e = jnp.exp2(jnp.minimum(b - back(b3, d), 0.0))
        a = jnp.sum(q * back(kk3, d) * e, axis=-1, keepdims=True)
        o = o + (a * validf[d]) * back(v3, d)

    bend = b3[:, HGRN_SUB - 1:HGRN_SUB, :]
    bstart = jnp.concatenate([jnp.zeros((1, 1, HEAD_W), F32), bend[:-1]], axis=0)
    qt = (q.reshape(nb, HGRN_SUB, HEAD_W) * jnp.exp2(b3 - bstart)).reshape(C, HEAD_W).astype(BF16)
    a_off = None
    for d in range(1, nb):
        bsh = jnp.concatenate([bend[d - 1:]] + [bend[nb - 1:nb]] * (d - 1), axis=0)
        kd = (kk3 * jnp.exp2(bsh - b3)).reshape(C, HEAD_W).astype(BF16)
        panel = lax.dot_general(qt, kd, (((1,), (1,)), ((), ())), preferred_element_type=F32)
        a_off = panel * bandf[d] if a_off is None else a_off + panel * bandf[d]
    vb = v.astype(BF16)
    o = o + jnp.dot(a_off.astype(BF16), vb, preferred_element_type=F32)

    qe = (q * jnp.exp2(b)).astype(BF16)
    o = o + lax.dot_general(qe, ST.astype(BF16), (((1,), (1,)), ((), ())), preferred_element_type=F32)
    b_end = b[C - 1:C, :]
    ke = (kk * jnp.exp2(b_end - b)).astype(BF16)
    ST_new = ST * jnp.exp2(b_end) + lax.dot_general(vb, ke, (((0,), (0,)), ((), ())),
                                                     preferred_element_type=F32)
    return o, ST_new


def _hgrn_body(q_ref, f_ref, i_ref, g_ref, lbp_ref, gh_ref, o_ref, st_ref, ST_scr, *, tb, C):
    it = pl.program_id(1)

    @pl.when(it == 0)
    def _():
        ST_scr[...] = jnp.zeros(ST_scr.shape, F32)

    lbp = lbp_ref[...]
    gh = gh_ref[...]
    consts = _hgrn_consts(C)

    def chunk_of(heads):
        w = slice(heads[0] * HEAD_W, (heads[-1] + 1) * HEAD_W)

        def chunk(ci, carry):
            rows = pl.ds(pl.multiple_of(ci * C, C), C)
            q = _silu(q_ref[rows, w])
            lg, kk = _hgrn_gates(f_ref[rows, w], lbp[0:1, w], lbp[1:2, w], lbp[2:3, w])
            b = jnp.dot(consts[0], lg * LOG2E, preferred_element_type=F32, precision=lax.Precision.HIGHEST)
            v = i_ref[rows, w]
            gate = _silu(g_ref[rows, w])
            for n, h in enumerate(heads):
                sl = slice(n * HEAD_W, (n + 1) * HEAD_W)
                o, ST_new = _hgrn_head(q[:, sl], kk[:, sl], v[:, sl], b[:, sl], ST_scr[h], consts, C)
                ST_scr[h] = ST_new
                ms = jnp.mean(o * o, axis=-1, keepdims=True)
                o_ref[rows, h * HEAD_W:(h + 1) * HEAD_W] = (
                    o * lax.rsqrt(ms + EPS) * gh * gate[:, sl]).astype(o_ref.dtype)
            return carry

        return chunk

    for heads in HGRN_HEAD_GROUPS:
        lax.fori_loop(0, tb // C, chunk_of(heads), 0, unroll=HGRN_UNROLL)

    @pl.when(it == pl.num_programs(1) - 1)
    def _():
        for h in range(A_HEADS):
            st_ref[h] = ST_scr[h].T


def _hgrn_prompt(z, lbp, gh, B, T, tb=512, C=HGRN_CHUNK):
    nt = T // tb

    def zspec(cb):
        return pl.BlockSpec((tb, 512), lambda b, i: (b * nt + i, cb))

    return pl.pallas_call(
        functools.partial(_hgrn_body, tb=tb, C=C),
        grid=(B, nt),
        in_specs=[zspec(ZP_HQ), zspec(ZP_HF), zspec(ZP_HI), zspec(ZP_HG),
                  _resident((3, 512)), _resident((1, HEAD_W))],
        out_specs=[
            pl.BlockSpec((tb, 512), lambda b, i: (b * nt + i, 0)),
            pl.BlockSpec((None, A_HEADS, HEAD_W, HEAD_W), lambda b, i: (b, 0, 0, 0)),
        ],
        out_shape=[
            jax.ShapeDtypeStruct((B * T, 512), BF16),
            jax.ShapeDtypeStruct((B, A_HEADS, HEAD_W, HEAD_W), F32),
        ],
        scratch_shapes=[pltpu.VMEM((A_HEADS, HEAD_W, HEAD_W), F32)],
        compiler_params=_cparams(("parallel", "arbitrary")),
        name="hgrn_prompt",
    )(z, z, z, z, lbp, gh)


def _merge_body(x_ref, a_ref, p_ref, o_ref, g0_ref, g1_ref, g2_ref, wb_ref, wo_ref, y_ref):
    merged = None
    for n, (br, gz) in enumerate(((a_ref, g0_ref), (p_ref, g1_ref), (o_ref, g2_ref))):
        proj = jnp.dot(br[...].astype(BF16), wb_ref[n], preferred_element_type=F32)
        term = _sigmoid(gz[...]) * proj
        merged = term if merged is None else merged + term
    y_ref[...] = x_ref[...] + jnp.dot(merged.astype(BF16), wo_ref[...], preferred_element_type=F32)


def _merge(x, a, p, o, zg, wb, wo, layer, tm):
    M, D = x.shape

    def gspec(n):
        return pl.BlockSpec((tm, D), lambda i: (i, n))

    bspec = pl.BlockSpec((tm, BRANCH_W), lambda i: (i, 0))
    return pl.pallas_call(
        _merge_body,
        grid=(M // tm,),
        in_specs=[pl.BlockSpec((tm, D), lambda i: (i, 0)), bspec, bspec, bspec,
                  gspec(0), gspec(1), gspec(2),
                  pl.BlockSpec((None, N_BRANCH, BRANCH_W, D), lambda i: (layer, 0, 0, 0)),
                  pl.BlockSpec((None, D, D), lambda i: (layer, 0, 0))],
        out_specs=pl.BlockSpec((tm, D), lambda i: (i, 0)),
        out_shape=jax.ShapeDtypeStruct((M, D), F32),
        compiler_params=_cparams(("parallel",)),
        name="merge",
    )(x, a, p, o, zg, zg, zg, wb, wo)


def _mlp_body(x_ref, g_ref, wu_ref, wd_ref, y_ref, *, fc):
    x = x_ref[...]
    ms = jnp.mean(x * x, axis=-1, keepdims=True)
    hm = (x * lax.rsqrt(ms + EPS) * g_ref[...]).astype(BF16)
    acc = x
    for c in range(wu_ref.shape[1] // fc):
        u = jnp.dot(hm, wu_ref[:, c * fc:(c + 1) * fc], preferred_element_type=F32)
        r = jnp.maximum(u, 0.0)
        acc = acc + jnp.dot((r * r).astype(BF16), wd_ref[c * fc:(c + 1) * fc, :], preferred_element_type=F32)
    y_ref[...] = acc


def _mlp(x, g, wu, wd, layer, tm, fc=1024):
    M, D = x.shape
    Fd = wu.shape[2]
    return pl.pallas_call(
        functools.partial(_mlp_body, fc=fc),
        grid=(M // tm,),
        in_specs=[pl.BlockSpec((tm, D), lambda i: (i, 0)), _resident((1, D)),
                  pl.BlockSpec((None, D, Fd), lambda i: (layer, 0, 0), pipeline_mode=pl.Buffered(1)),
                  pl.BlockSpec((None, Fd, D), lambda i: (layer, 0, 0), pipeline_mode=pl.Buffered(1))],
        out_specs=pl.BlockSpec((tm, D), lambda i: (i, 0)),
        out_shape=jax.ShapeDtypeStruct((M, D), F32),
        compiler_params=_cparams(("parallel",)),
        name="mlp",
    )(x, g, wu, wd)


def _sprep_body(q_ref, k_ref, gq_ref, gk_ref, qn_ref, kT_ref, ss_ref):
    qnT = _group_norm_T(q_ref[...].T, gq_ref[...]) * (A_QK_DIM ** -0.5)
    knT = _group_norm_T(k_ref[...].T, gk_ref[...])
    qn_ref[...] = qnT
    kT_ref[...] = knT
    ss_ref[...] = jnp.sum((qnT * knT).reshape(8, A_QK_DIM, qnT.shape[1]), axis=1)


def _sprep(zs, gq_col, gk_col):
    Bs = zs.shape[0]
    return pl.pallas_call(
        _sprep_body,
        grid=(1,),
        in_specs=[pl.BlockSpec((Bs, 512), lambda i: (0, ZC_Q)),
                  pl.BlockSpec((Bs, 512), lambda i: (0, ZC_K)),
                  _resident((512, 1)), _resident((512, 1))],
        out_specs=[_resident((512, Bs)), _resident((512, Bs)), _resident((8, Bs))],
        out_shape=[jax.ShapeDtypeStruct((512, Bs), F32),
                   jax.ShapeDtypeStruct((512, Bs), F32),
                   jax.ShapeDtypeStruct((8, Bs), F32)],
        compiler_params=_cparams(("arbitrary",)),
        name="sample_prep",
    )(zs, zs, gq_col, gk_col)


def _decode_body(pt_ref, lam_ref, qn_ref, ss_ref, v_ref, bias_ref, gs_ref, *rest, npages, out_scale):
    del pt_ref
    _decode_seq(pl.program_id(0), lam_ref[0], qn_ref, ss_ref, v_ref, bias_ref, gs_ref,
                rest[:npages], rest[npages:2 * npages], rest[2 * npages], out_scale)


def _decode_seq(b, lam, qn_ref, ss_ref, v_ref, bias_ref, gs_ref, k_refs, v_refs, o_ref, out_scale):
    npages = len(k_refs)
    Bs = qn_ref.shape[1]

    lane_q = lax.broadcasted_iota(jnp.int32, qn_ref.shape, 1)
    q_col = jnp.sum(jnp.where(lane_q == b, qn_ref[...], 0.0), axis=-1, keepdims=True)
    q_b = jnp.broadcast_to(q_col, (8 * A_QK_DIM, PAGE))
    s_pages = [jnp.sum((k_refs[p][...] * q_b).reshape(8, A_QK_DIM, PAGE), axis=1)
               for p in range(npages)]
    s_pages[-1] = s_pages[-1] + bias_ref[:, 0:PAGE]
    lane_b = lax.broadcasted_iota(jnp.int32, (8, Bs), 1)
    s_self = (jnp.sum(jnp.where(lane_b == b, ss_ref[...], 0.0), axis=-1, keepdims=True)
              + bias_ref[:, PAGE:PAGE + 1])

    m = s_self
    for s in s_pages:
        m = jnp.maximum(m, jnp.max(s, axis=-1, keepdims=True))
    p_self = jnp.exp(s_self - m)
    p_pages = [jnp.exp(s - m) for s in s_pages]
    l = p_self
    for p in p_pages:
        l = l + jnp.sum(p, axis=-1, keepdims=True)
    row1 = lax.broadcasted_iota(jnp.int32, (8, 1), 0)
    coef = jnp.where((row1 & 1) == 0, 1.0, -lam) / l

    def pair(x):
        return x + pltpu.roll(x, 7, 0)

    w_self = pair(jnp.broadcast_to(p_self * coef, (8, HEAD_W)))
    w_all = jnp.concatenate([pair(p_pages[pg] * coef) for pg in range(npages)], axis=0)
    rr = lax.broadcasted_iota(jnp.int32, (PAGE, A_HEADS * PAGE), 0)
    cc = lax.broadcasted_iota(jnp.int32, (PAGE, A_HEADS * PAGE), 1)
    spread = ((cc >> 2) == rr).astype(BF16)
    w_exp = jnp.dot(w_all.astype(BF16), spread, preferred_element_type=F32)
    row_h = lax.broadcasted_iota(jnp.int32, (8, A_HEADS * PAGE), 0) >> 1
    own = (lax.broadcasted_iota(jnp.int32, (8, A_HEADS * PAGE), 1) & (A_HEADS - 1)) == row_h
    acc = jnp.zeros((8, HEAD_W), F32)
    for pg in range(npages):
        wp = jnp.where(own, w_exp[8 * pg:8 * pg + 8, :], 0.0).astype(BF16)
        acc = acc + jnp.dot(wp, v_refs[pg][...].astype(BF16), preferred_element_type=F32)
    v_new = v_ref[pl.ds(b, 1), :]
    outs = []
    for h in range(A_HEADS):
        oh = acc[2 * h:2 * h + 1, :] + w_self[2 * h:2 * h + 1, :] * v_new[:, h * HEAD_W:(h + 1) * HEAD_W]
        ms = jnp.mean(oh * oh, axis=-1, keepdims=True)
        outs.append(oh * lax.rsqrt(ms + EPS) * gs_ref[...] * out_scale)
    o_ref[pl.ds(b, 1), :] = jnp.concatenate(outs, axis=1)


def _decode(pt_flat, lam, qn, ss, zs, bias_dec, gs, ckT, cv, layer, npages, out_scale):
    Bs = qn.shape[1]

    def page_spec(p):
        return pl.BlockSpec((None, None, 512, PAGE),
                            lambda b, pt, p=p: (layer, pt[b * npages + p], 0, 0))

    in_specs = [
        pl.BlockSpec(memory_space=pltpu.SMEM),
        pl.BlockSpec((512, Bs), lambda b, pt: (0, 0)),
        pl.BlockSpec((8, Bs), lambda b, pt: (0, 0)),
        pl.BlockSpec((Bs, 512), lambda b, pt: (0, ZC_V)),
        pl.BlockSpec((8, 2 * PAGE), lambda b, pt: (0, 0)),
        pl.BlockSpec((1, HEAD_W), lambda b, pt: (0, 0)),
    ] + [page_spec(p) for p in range(npages)] * 2
    return pl.pallas_call(
        functools.partial(_decode_body, npages=npages, out_scale=out_scale),
        grid_spec=pltpu.PrefetchScalarGridSpec(
            num_scalar_prefetch=1,
            grid=(Bs,),
            in_specs=in_specs,
            out_specs=pl.BlockSpec((Bs, 512), lambda b, pt: (0, 0)),
        ),
        out_shape=jax.ShapeDtypeStruct((Bs, 512), F32),
        compiler_params=_cparams(("arbitrary",)),
        name="decode_attn",
    )(pt_flat, lam, qn, ss, zs, bias_dec, gs, *([ckT] * npages), *([cv] * npages))


def _smix_body(q_ref, f_ref, i_ref, g_ref, u_ref, sp_ref, s0_ref, lbp_ref, gh_ref, wp_ref, sc_ref, s1_in,
               o_ref, p_ref, spn_ref, s1_ref, o_scr, *, ns, past_len):
    del s1_in
    q = _silu(q_ref[...])
    lg, kk = _hgrn_gates(f_ref[...], lbp_ref[0:1, :], lbp_ref[1:2, :], lbp_ref[2:3, :])
    g = jnp.exp(lg)
    v = i_ref[...]
    eye = (lax.broadcasted_iota(jnp.int32, (HEAD_W, HEAD_W), 0)
           == lax.broadcasted_iota(jnp.int32, (HEAD_W, HEAD_W), 1))

    def col_of(row):
        return jnp.sum(jnp.where(eye, jnp.broadcast_to(row, eye.shape), 0.0), axis=-1, keepdims=True)

    for s in range(ns):
        for h in range(A_HEADS):
            sl = slice(h * HEAD_W, (h + 1) * HEAD_W)
            S_new = (col_of(g[s:s + 1, sl]) * s0_ref[s, h]
                     + col_of(kk[s:s + 1, sl]) * v[s:s + 1, sl])
            s1_ref[s, h] = S_new
            o_scr[s:s + 1, sl] = jnp.sum(col_of(q[s:s + 1, sl]) * S_new, axis=0, keepdims=True)
    zg = g_ref[...]
    for h in range(A_HEADS):
        sl = slice(h * HEAD_W, (h + 1) * HEAD_W)
        oh = o_scr[:, sl]
        ms = jnp.mean(oh * oh, axis=-1, keepdims=True)
        o_ref[:, sl] = oh * lax.rsqrt(ms + EPS) * gh_ref[...] * _silu(zg[:, sl])

    u = u_ref[...]
    for g_i, win in enumerate(P_WINDOWS):
        sl = slice(g_i * HEAD_W, (g_i + 1) * HEAD_W)
        ug = u[:, sl]
        sacc = ug
        for j in range(1, win):
            sacc = sacc + sp_ref[POOL_BUF - j][:, sl]
        cnt = float(min(past_len + 1, win))
        pooled = sacc / cnt - ug
        p_ref[:, sl] = (jnp.dot(pooled.astype(BF16), wp_ref[g_i], preferred_element_type=F32)
                        * sc_ref[:, sl])
    for j in range(POOL_BUF - 1):
        spn_ref[j] = sp_ref[j + 1]
    spn_ref[POOL_BUF - 1] = u


def _smix(zs, sp_t, s0, lbp, gh, w_pool, sc, s1_all, layer, past_len, ns=8):
    Bs = zs.shape[0]

    def zspec(cb):
        return pl.BlockSpec((ns, 512), lambda i: (i, cb))

    return pl.pallas_call(
        functools.partial(_smix_body, ns=ns, past_len=past_len),
        grid=(Bs // ns,),
        in_specs=[zspec(ZC_HQ), zspec(ZC_HF), zspec(ZC_HI), zspec(ZC_HG), zspec(ZC_U),
                  pl.BlockSpec((None, POOL_BUF, ns, 512), lambda i: (layer, 0, i, 0)),
                  pl.BlockSpec((None, ns, A_HEADS, HEAD_W, HEAD_W), lambda i: (layer, i, 0, 0, 0)),
                  _resident((3, 512)), _resident((1, HEAD_W)),
                  pl.BlockSpec((None, 4, HEAD_W, HEAD_W), lambda i: (layer, 0, 0, 0)),
                  _resident((1, 512)),
                  pl.BlockSpec(memory_space=pl.ANY)],
        out_specs=[pl.BlockSpec((ns, 512), lambda i: (i, 0)),
                   pl.BlockSpec((ns, 512), lambda i: (i, 0)),
                   pl.BlockSpec((POOL_BUF, ns, 512), lambda i: (0, i, 0)),
                   pl.BlockSpec((None, ns, A_HEADS, HEAD_W, HEAD_W), lambda i: (layer, i, 0, 0, 0))],
        out_shape=[jax.ShapeDtypeStruct((Bs, 512), F32),
                   jax.ShapeDtypeStruct((Bs, 512), F32),
                   jax.ShapeDtypeStruct((POOL_BUF, Bs, 512), F32),
                   jax.ShapeDtypeStruct(s1_all.shape, F32)],
        input_output_aliases={11: 3},
        scratch_shapes=[pltpu.VMEM((ns, 512), F32)],
        compiler_params=_cparams(("parallel",)),
        name="sample_mix",
    )(zs, zs, zs, zs, zs, sp_t, s0, lbp, gh, w_pool, sc, s1_all)


def kernel(x_prompt, x_sample, cache_k, cache_v, state_pool, state_hgrn, page_table, rel_table, lb_param, w_in, g_mix, g_q, g_k, lam_p, g_sub, w_pool, pool_scale, g_h, w_branch, w_out, g_mlp, w_up, w_down):
    B, T, D = x_prompt.shape
    Bs = x_sample.shape[0]
    depth, n_phys = cache_k.shape[:2]
    npages = page_table.shape[1]
    past_len = npages * PAGE
    tile = ATT_TILE
    att_steps = B * A_HEADS * (T // tile)
    spg = Bs // att_steps if Bs % att_steps == 0 else 0

    lb_all = jnp.cumsum(jax.nn.softmax(lb_param.astype(F32), axis=0), axis=0)
    lb_all = lb_all - lb_all[:1]
    lbp_all = jnp.stack([jnp.log(lb_all), jnp.log1p(-lb_all), 1.0 - lb_all], axis=1)
    lp = lam_p.astype(F32)
    lam_dyn = jnp.exp(jnp.sum(lp[:, 0] * lp[:, 1], axis=-1)) - jnp.exp(jnp.sum(lp[:, 2] * lp[:, 3], axis=-1))
    w_mix_b = w_in[:, :, :MIX_W].astype(BF16)
    w_gate_b = w_in[:, :, MIX_W:].astype(BF16)
    w_pool_b, w_branch_b = w_pool.astype(BF16), w_branch.astype(BF16)
    w_out_b, w_up_b, w_down_b = w_out.astype(BF16), w_up.astype(BF16), w_down.astype(BF16)

    rr = np.arange(tile)[:, None]
    cc = np.arange(2 * tile)[None, :]
    dist = rr + tile - cc
    bkt_prompt = np.where(dist >= 0, _rel_bucket_np(np.maximum(dist, 0)), -1).astype(np.int32)
    bias_prompt = _bias_tiles(rel_table, bkt_prompt, LOG2E)
    dd = np.concatenate([PAGE - np.arange(PAGE), np.zeros(PAGE, np.int64)])
    bkt_dec = np.broadcast_to(_rel_bucket_np(dd)[None, :], (8, 2 * PAGE)).astype(np.int32)
    bias_dec_h = _bias_tiles(rel_table, bkt_dec, 1.0)
    bias_dec = jnp.repeat(bias_dec_h[:, 0, :], 2, axis=0)

    tab2 = (rel_table.astype(F32) - rel_table[REL_BUCKETS - 1:].astype(F32)) * LOG2E
    bias_hi = jnp.maximum(jnp.max(tab2, axis=0), 0.0)
    bias_lo = jnp.minimum(jnp.min(tab2, axis=0), 0.0)
    slack = 1.01
    k_bound = 8.0 * slack * jnp.max(jnp.abs(g_k.astype(F32)), axis=-1)
    q_bound = 8.0 * slack * (A_QK_DIM ** -0.5 * LOG2E) * jnp.max(jnp.abs(g_q.astype(F32)), axis=-1)
    spread = 2.0 * q_bound * k_bound + jnp.max(bias_hi - bias_lo)
    bounded = (spread <= ATT_MAX_SPREAD).astype(F32)

    ckT = jnp.transpose(cache_k, (0, 1, 3, 4, 5, 2)).reshape(depth, n_phys, 512, PAGE)
    cv4 = cache_v.reshape(depth, n_phys, A_HEADS * PAGE, HEAD_W)
    sp_t = jnp.transpose(state_pool, (0, 2, 1, 3))
    pt_flat = page_table.reshape(-1).astype(jnp.int32)

    xp = x_prompt.reshape(B * T, D)
    xs = x_sample.reshape(Bs, D)
    ks_l, vs_l, pp_l, ps_l, sp_l = [], [], [], [], []
    kT_all = jnp.zeros((depth, B, 512, T), F32)
    v4_all = jnp.zeros((depth, B, A_HEADS * T, HEAD_W), F32)
    s1_all = jnp.zeros((depth, Bs, A_HEADS, HEAD_W, HEAD_W), F32)
    for l in range(depth):
        lam_init = 0.8 - 0.6 * math.exp(-0.3 * l)
        out_scale = 1.0 - lam_init
        lam = (lam_dyn[l] + lam_init).reshape(1).astype(F32)
        par = jnp.concatenate([lam, bounded[l].reshape(1), k_bound[l].reshape(1), bias_hi]).astype(F32)
        gmix = g_mix[l].reshape(1, D)
        gq_col = jnp.tile(g_q[l], 8).reshape(512, 1)
        gk_col = jnp.tile(g_k[l], 8).reshape(512, 1)
        gq2 = jnp.tile(g_q[l], 2).reshape(1, HEAD_W)
        gs = g_sub[l].reshape(1, HEAD_W)
        gh = g_h[l].reshape(1, HEAD_W)
        sc = pool_scale[l].reshape(1, 512)
        gmlp = g_mlp[l].reshape(1, D)
        lbp = lbp_all[l]

        zs = _inproj(xs, gmix, w_mix_b, l, tm=Bs)
        zgs = _inproj(xs, gmix, w_gate_b, l, tm=Bs)
        qn, kTs, ssf = _sprep(zs, gq_col, gk_col)
        z, kT_all, kTb, v4_all, vb1 = _inproj_prompt(xp, gmix, w_mix_b, gk_col, kT_all, v4_all, l, B, T)
        zg = _inproj(xp, gmix, w_gate_b, l, tm=512)

        a, a_s = _attn_prompt(pt_flat, par, z, kTb, vb1, bias_prompt, gq2, gs, qn, ssf, zs, bias_dec,
                              ckT, cv4, l, B, T, out_scale, npages, spg)
        if spg == 0:
            a_s = _decode(pt_flat, lam, qn, ssf, zs, bias_dec, gs, ckT, cv4, l, npages, out_scale)

        p = _pool_prompt(z, w_pool_b, sc, l, B, T)
        o, st = _hgrn_prompt(z, lbp, gh, B, T)
        x1 = _merge(xp, a, p, o, zg, w_branch_b, w_out_b, l, tm=512)
        xp = _mlp(x1, gmlp, w_up_b, w_down_b, l, tm=512)
        pp_l.append(z.reshape(B, T, -1)[:, T - POOL_BUF:, ZP_U * 512:(ZP_U + 1) * 512])
        sp_l.append(st)

        o_s, p_s, spn, s1_all = _smix(zs, sp_t, state_hgrn, lbp, gh, w_pool_b, sc, s1_all, l, past_len)
        x1s = _merge(xs, a_s, p_s, o_s, zgs, w_branch_b, w_out_b, l, tm=Bs)
        xs = _mlp(x1s, gmlp, w_up_b, w_down_b, l, tm=Bs)
        ks_l.append(kTs)
        vs_l.append(zs[:, ZC_V * 512:(ZC_V + 1) * 512])
        ps_l.append(spn)

    k_prompt = jnp.transpose(kT_all.reshape(depth, B, A_HEADS, 2, A_QK_DIM, T), (0, 1, 5, 2, 3, 4))
    v_prompt = v4_all.reshape(depth, B, T, A_HEADS, A_V_DIM)
    k_sample = jnp.transpose(jnp.stack(ks_l).reshape(depth, A_HEADS, 2, A_QK_DIM, Bs), (0, 4, 1, 2, 3))[:, :, None]
    v_sample = jnp.stack(vs_l).reshape(depth, Bs, 1, A_HEADS, A_V_DIM)
    pool_prompt = jnp.stack(pp_l)
    pool_sample = jnp.transpose(jnp.stack(ps_l), (0, 2, 1, 3))
    return (xp.reshape(B, T, D), xs.reshape(Bs, 1, D), k_prompt, v_prompt, k_sample, v_sample,
            pool_prompt, pool_sample, jnp.stack(sp_l), s1_all)
```

```python
import functools
import math

import numpy as np
import jax
import jax.numpy as jnp
from jax import lax
from jax.experimental import pallas as pl
from jax.experimental.pallas import tpu as pltpu

F32 = jnp.float32
BF16 = jnp.bfloat16
EPS = 1e-6

A_HEADS = 4
A_QK_DIM = 64
A_V_DIM = 128
HEAD_W = 128
P_WINDOWS = (2, 4, 8, 16)
POOL_BUF = 15
N_BRANCH = 3
BRANCH_W = 512
REL_BUCKETS = 32
REL_MAX_DIST = 128
PAGE = 128
NEG = -1e30
LOG2E = math.log2(math.e)

ZC_Q, ZC_K, ZC_V, ZC_U, ZC_HQ, ZC_HF, ZC_HI, ZC_HG = range(8)
MIX_W = 8 * 512
ZP_Q, ZP_U, ZP_HQ, ZP_HF, ZP_HI, ZP_HG = range(6)

VMEM_LIMIT = 48 * 1024 * 1024
ATT_VMEM_LIMIT = 56 * 1024 * 1024
ATT_TILE = 512
ATT_MAX_SPREAD = 100.0
HGRN_CHUNK = 64
HGRN_SUB = 8
HGRN_HEAD_GROUPS = ((0, 1, 2, 3),)
HGRN_UNROLL = 8


def _cparams(sem):
    return pltpu.CompilerParams(dimension_semantics=sem, vmem_limit_bytes=VMEM_LIMIT)


def _resident(shape):
    nd = len(shape)
    return pl.BlockSpec(shape, lambda *_: (0,) * nd)


def _sigmoid(x):
    return 0.5 * jnp.tanh(0.5 * x) + 0.5


def _silu(x):
    return x * _sigmoid(x)


def _normed(x_ref, g_ref):
    x = x_ref[...]
    ms = jnp.mean(x * x, axis=-1, keepdims=True)
    return (x * lax.rsqrt(ms + EPS) * g_ref[...]).astype(BF16)


def _inproj_body(x_ref, g_ref, w_ref, z_ref):
    z_ref[...] = jnp.dot(_normed(x_ref, g_ref), w_ref[...], preferred_element_type=F32)


def _inproj(x, g, w, layer, tm):
    M, D = x.shape
    N = w.shape[2]
    return pl.pallas_call(
        _inproj_body,
        grid=(M // tm,),
        in_specs=[
            pl.BlockSpec((tm, D), lambda i: (i, 0)),
            pl.BlockSpec((1, D), lambda i: (0, 0)),
            pl.BlockSpec((None, D, N), lambda i: (layer, 0, 0), pipeline_mode=pl.Buffered(1)),
        ],
        out_specs=pl.BlockSpec((tm, N), lambda i: (i, 0)),
        out_shape=jax.ShapeDtypeStruct((M, N), F32),
        compiler_params=_cparams(("parallel",)),
        name="inproj",
    )(x, g, w)


def _inproj_prompt_body(x_ref, g_ref, w_ref, gk_ref, kT_in, v4_in, zp_ref, kT_ref, kTb_ref, v4_ref,
                        vb1_ref, k_scr, *, tm):
    del kT_in, v4_in
    h = _normed(x_ref, g_ref)

    def cols(c0, c1):
        return jnp.dot(h, w_ref[:, c0 * 512:c1 * 512], preferred_element_type=F32)

    zp_ref[:, 0:512] = cols(ZC_Q, ZC_Q + 1)
    k_scr[...] = cols(ZC_K, ZC_K + 1)
    kn = _group_norm_T(k_scr[...].T, gk_ref[...])
    kT_ref[...] = kn
    kTb_ref[...] = kn.astype(BF16)
    v = cols(ZC_V, ZC_V + 1)
    ones = jnp.ones((tm, HEAD_W), BF16)
    for hd in range(A_HEADS):
        vh = v[:, hd * HEAD_W:(hd + 1) * HEAD_W]
        v4_ref[pl.ds(hd, tm, stride=A_HEADS), :] = vh
        vb1_ref[:, 2 * hd * HEAD_W:(2 * hd + 1) * HEAD_W] = vh.astype(BF16)
        vb1_ref[:, (2 * hd + 1) * HEAD_W:(2 * hd + 2) * HEAD_W] = ones
    for n, c in enumerate(range(ZC_U, ZC_HG + 1)):
        zp_ref[:, (n + 1) * 512:(n + 2) * 512] = cols(c, c + 1)


def _inproj_prompt(x, g, w, gk_col, kT_all, v4_all, layer, B, T, tm=512):
    M, D = x.shape
    N = w.shape[2]
    nt = T // tm
    return pl.pallas_call(
        functools.partial(_inproj_prompt_body, tm=tm),
        grid=(B, nt),
        in_specs=[
            pl.BlockSpec((tm, D), lambda b, i: (b * nt + i, 0)),
            pl.BlockSpec((1, D), lambda b, i: (0, 0)),
            pl.BlockSpec((None, D, N), lambda b, i: (layer, 0, 0), pipeline_mode=pl.Buffered(1)),
            pl.BlockSpec((512, 1), lambda b, i: (0, 0)),
            pl.BlockSpec(memory_space=pl.ANY),
            pl.BlockSpec(memory_space=pl.ANY),
        ],
        out_specs=[
            pl.BlockSpec((tm, 6 * 512), lambda b, i: (b * nt + i, 0)),
            pl.BlockSpec((None, None, 512, tm), lambda b, i: (layer, b, 0, i)),
            pl.BlockSpec((None, 512, tm), lambda b, i: (b, 0, i)),
            pl.BlockSpec((None, None, A_HEADS * tm, HEAD_W), lambda b, i: (layer, b, i, 0)),
            pl.BlockSpec((tm, 2 * 512), lambda b, i: (b * nt + i, 0)),
        ],
        out_shape=[
            jax.ShapeDtypeStruct((M, 6 * 512), F32),
            jax.ShapeDtypeStruct(kT_all.shape, F32),
            jax.ShapeDtypeStruct((B, 512, T), BF16),
            jax.ShapeDtypeStruct(v4_all.shape, F32),
            jax.ShapeDtypeStruct((M, 2 * 512), BF16),
        ],
        input_output_aliases={4: 1, 5: 3},
        scratch_shapes=[pltpu.VMEM((tm, 512), F32)],
        compiler_params=_cparams(("parallel", "parallel")),
        name="inproj_prompt",
    )(x, g, w, gk_col, kT_all, v4_all)


def _group_norm_T(xT, gcol):
    n = xT.shape[1]
    x3 = xT.reshape(8, A_QK_DIM, n)
    ms = jnp.mean(x3 * x3, axis=1, keepdims=True)
    return (x3 * lax.rsqrt(ms + EPS)).reshape(8 * A_QK_DIM, n) * gcol


def _rel_bucket_np(n):
    n = np.asarray(n, np.int32)
    max_exact = REL_BUCKETS // 2
    nf = np.maximum(n, max_exact).astype(np.float32)
    large = max_exact + (np.log(nf / np.float32(max_exact)) / np.float32(math.log(REL_MAX_DIST / max_exact))
                         * np.float32(REL_BUCKETS - max_exact)).astype(np.int32)
    large = np.minimum(large, REL_BUCKETS - 1)
    return np.where(n < max_exact, n, large).astype(np.int32)


def _bias_body(tab_ref, bkt_ref, o_ref, *, scale):
    h = pl.program_id(0)
    bkt = bkt_ref[...]
    far = tab_ref[REL_BUCKETS - 1, h]
    acc = jnp.full(bkt.shape, NEG, F32)
    for b in range(REL_BUCKETS):
        acc = jnp.where(bkt == b, (tab_ref[b, h] - far) * scale, acc)
    o_ref[...] = acc


def _bias_tiles(rel_table, buckets, scale):
    R, C = buckets.shape
    return pl.pallas_call(
        functools.partial(_bias_body, scale=scale),
        grid=(A_HEADS,),
        in_specs=[pl.BlockSpec(memory_space=pltpu.SMEM), _resident((R, C))],
        out_specs=pl.BlockSpec((None, R, C), lambda h: (h, 0, 0)),
        out_shape=jax.ShapeDtypeStruct((A_HEADS, R, C), F32),
        compiler_params=_cparams(("arbitrary",)),
        name="bias_tiles",
    )(rel_table, jnp.asarray(buckets))


def _attn_body(pt_ref, par_ref, q_ref, kT_ref, v1_ref, bias_ref, gq_ref, gs_ref,
               qn_ref, ss_ref, vnew_ref, biasd_ref, *rest, tile, out_scale, npages, spg):
    del pt_ref
    k_refs = rest[:spg * npages]
    v_refs = rest[spg * npages:2 * spg * npages]
    o_ref, od_ref, mb_scr, m_scr, acc_scr = rest[2 * spg * npages:]
    h = pl.program_id(1)
    qi = pl.program_id(2)
    step_id = (pl.program_id(0) * pl.num_programs(1) + h) * pl.num_programs(2) + qi

    def decode():
        for j in range(spg):
            _decode_seq(step_id * spg + j, par_ref[0], qn_ref, ss_ref, vnew_ref, biasd_ref, gs_ref,
                        k_refs[j * npages:(j + 1) * npages], v_refs[j * npages:(j + 1) * npages],
                        od_ref, out_scale)

    q = q_ref[...]
    lane = lax.broadcasted_iota(jnp.int32, q.shape, 1)
    lo = lane < A_QK_DIM
    q2 = q * q
    ms0 = jnp.sum(jnp.where(lo, q2, 0.0), axis=-1, keepdims=True) * (1.0 / A_QK_DIM)
    ms1 = jnp.sum(jnp.where(lo, 0.0, q2), axis=-1, keepdims=True) * (1.0 / A_QK_DIM)
    rs = jnp.where(lo, lax.rsqrt(ms0 + EPS), lax.rsqrt(ms1 + EPS))
    qn = q * rs * gq_ref[...] * (A_QK_DIM ** -0.5 * LOG2E)
    qf = (jnp.where(lo, qn, 0.0), jnp.where(lo, 0.0, qn))
    qs = (qf[0].astype(BF16), qf[1].astype(BF16))
    acc_scr[...] = jnp.zeros(acc_scr.shape, F32)
    diag = pl.multiple_of(qi * tile, tile)
    sub = pl.multiple_of(jnp.maximum(qi - 1, 0) * tile, tile)

    n_far = jnp.maximum(qi - 1, 0)

    @pl.when(par_ref[1] > 0.5)
    def _():
        for mp in range(2):
            nq = jnp.sqrt(jnp.sum(qf[mp] * qf[mp], axis=-1, keepdims=True))
            mb_scr[mp] = jnp.broadcast_to(nq * par_ref[2] + par_ref[3 + h], (tile, tile))

        def pv(ks, bias):
            kt = kT_ref[:, pl.ds(ks, tile)]
            vt = v1_ref[pl.ds(ks, tile), :]
            out = []
            for mp in range(2):
                s = jnp.dot(qs[mp], kt, preferred_element_type=F32)
                if bias is not None:
                    s = s + bias
                p = jnp.exp2(s - mb_scr[mp]).astype(BF16)
                out.append(jnp.dot(p, vt, preferred_element_type=F32))
            return out

        def far_pair(i, carry):
            c0 = pv(pl.multiple_of(2 * i * tile, tile), None)
            c1 = pv(pl.multiple_of((2 * i + 1) * tile, tile), None)
            for mp in range(2):
                acc_scr[mp] = acc_scr[mp] + c0[mp] + c1[mp]
            return carry

        lax.fori_loop(0, n_far >> 1, far_pair, 0)

        @pl.when((n_far & 1) == 1)
        def _():
            c = pv(pl.multiple_of((n_far - 1) * tile, tile), None)
            for mp in range(2):
                acc_scr[mp] += c[mp]

        @pl.when(qi >= 1)
        def _():
            c0 = pv(sub, bias_ref[:, 0:tile])
            c1 = pv(diag, bias_ref[:, tile:2 * tile])
            decode()
            for mp in range(2):
                acc_scr[mp] = acc_scr[mp] + c0[mp] + c1[mp]

        @pl.when(qi == 0)
        def _():
            c = pv(diag, bias_ref[:, tile:2 * tile])
            decode()
            for mp in range(2):
                acc_scr[mp] += c[mp]

    @pl.when(par_ref[1] <= 0.5)
    def _():
        m_scr[...] = jnp.full(m_scr.shape, -jnp.inf, F32)

        def step(ks, bias):
            kt = kT_ref[:, pl.ds(ks, tile)]
            vt = v1_ref[pl.ds(ks, tile), :]
            for mp in range(2):
                s = jnp.dot(qs[mp], kt, preferred_element_type=F32)
                if bias is not None:
                    s = s + bias
                m_prev = m_scr[mp]
                m_new = jnp.maximum(m_prev, jnp.max(s, axis=-1, keepdims=True))
                p = jnp.exp2(s - m_new).astype(BF16)
                acc_scr[mp] = (jnp.exp2(m_prev - m_new) * acc_scr[mp]
                               + jnp.dot(p, vt, preferred_element_type=F32))
                m_scr[mp] = m_new

        def far_step(i, carry):
            step(pl.multiple_of(i * tile, tile), None)
            return carry

        lax.fori_loop(0, n_far, far_step, 0)

        @pl.when(qi >= 1)
        def _():
            step(sub, bias_ref[:, 0:tile])

        step(diag, bias_ref[:, tile:2 * tile])
        decode()

    a0 = acc_scr[0]
    a1 = acc_scr[1]
    o = a0[:, :HEAD_W] / a0[:, HEAD_W:] - par_ref[0] * (a1[:, :HEAD_W] / a1[:, HEAD_W:])
    ms = jnp.mean(o * o, axis=-1, keepdims=True)
    o_ref[...] = (o * lax.rsqrt(ms + EPS) * gs_ref[...] * out_scale).astype(o_ref.dtype)


def _attn_prompt(pt_flat, par, z, kTb, vb1, bias, gq2, gs, qn, ss, zs, bias_dec, ckT, cv, layer,
                 B, T, out_scale, npages, spg, tile=ATT_TILE):
    nq = T // tile
    M = B * T
    Bs = qn.shape[1]
    once = pl.Buffered(1)

    def page_spec(j, p):
        def imap(b, h, i, pt):
            seq = ((b * A_HEADS + h) * nq + i) * spg + j
            return (layer, pt[seq * npages + p], 0, 0)
        return pl.BlockSpec((None, None, 512, PAGE), imap)

    pages = [page_spec(j, p) for j in range(spg) for p in range(npages)]
    in_specs = [
        pl.BlockSpec(memory_space=pltpu.SMEM),
        pl.BlockSpec((tile, HEAD_W), lambda b, h, i, pt: (b * nq + i, ZP_Q * 4 + h)),
        pl.BlockSpec((None, HEAD_W, T), lambda b, h, i, pt: (b, h, 0), pipeline_mode=once),
        pl.BlockSpec((T, 2 * HEAD_W), lambda b, h, i, pt: (b, h), pipeline_mode=once),
        pl.BlockSpec((None, tile, 2 * tile), lambda b, h, i, pt: (h, 0, 0), pipeline_mode=once),
        pl.BlockSpec((1, HEAD_W), lambda b, h, i, pt: (0, 0)),
        pl.BlockSpec((1, HEAD_W), lambda b, h, i, pt: (0, 0)),
        pl.BlockSpec((512, Bs), lambda b, h, i, pt: (0, 0)),
        pl.BlockSpec((8, Bs), lambda b, h, i, pt: (0, 0)),
        pl.BlockSpec((Bs, 512), lambda b, h, i, pt: (0, ZC_V)),
        pl.BlockSpec((8, 2 * PAGE), lambda b, h, i, pt: (0, 0)),
    ] + pages + pages
    return pl.pallas_call(
        functools.partial(_attn_body, tile=tile, out_scale=out_scale, npages=npages, spg=spg),
        grid_spec=pltpu.PrefetchScalarGridSpec(
            num_scalar_prefetch=1,
            grid=(B, A_HEADS, nq),
            in_specs=in_specs,
            out_specs=[
                pl.BlockSpec((tile, HEAD_W), lambda b, h, i, pt: (b * nq + i, h)),
                pl.BlockSpec((Bs, 512), lambda b, h, i, pt: (0, 0)),
            ],
            scratch_shapes=[
                pltpu.VMEM((2, tile, tile), F32),
                pltpu.VMEM((2, tile, 1), F32),
                pltpu.VMEM((2, tile, 2 * HEAD_W), F32),
            ],
        ),
        out_shape=[jax.ShapeDtypeStruct((M, A_HEADS * HEAD_W), BF16),
                   jax.ShapeDtypeStruct((Bs, 512), F32)],
        compiler_params=pltpu.CompilerParams(
            dimension_semantics=("arbitrary", "arbitrary", "arbitrary"),
            vmem_limit_bytes=ATT_VMEM_LIMIT),
        name="attn_prompt",
    )(pt_flat, par, z, kTb, vb1, bias, gq2, gs, qn, ss, zs, bias_dec,
      *([ckT] * (spg * npages)), *([cv] * (spg * npages)))


def _pool_mix(ext_ref, base, n, pos, w_ref, sc_ref, o_ref):
    for g, win in enumerate(P_WINDOWS):
        sl = slice(g * HEAD_W, (g + 1) * HEAD_W)
        ug = ext_ref[base:base + n, sl]
        s = ug
        for j in range(1, win):
            s = s + ext_ref[base - j:base - j + n, sl]
        cnt = jnp.minimum(pos + 1, win).astype(F32)
        pooled = s / cnt - ug
        mixed = jnp.dot(pooled.astype(BF16), w_ref[g], preferred_element_type=F32) * sc_ref[:, sl]
        o_ref[:, sl] = mixed.astype(o_ref.dtype)


def _pool_body(u_ref, w_ref, sc_ref, o_ref, ext_scr, *, tm):
    i = pl.program_id(1)
    halo = 16

    @pl.when(i == 0)
    def _():
        ext_scr[0:halo, :] = jnp.zeros((halo, ext_scr.shape[1]), F32)

    @pl.when(i > 0)
    def _():
        ext_scr[0:halo, :] = ext_scr[tm:tm + halo, :]

    ext_scr[halo:halo + tm, :] = u_ref[...]
    pos = i * tm + lax.broadcasted_iota(jnp.int32, (tm, 1), 0)
    _pool_mix(ext_scr, halo, tm, pos, w_ref, sc_ref, o_ref)


def _pool_prompt(z, w_pool, sc, layer, B, T, tm=512):
    nt = T // tm
    return pl.pallas_call(
        functools.partial(_pool_body, tm=tm),
        grid=(B, nt),
        in_specs=[
            pl.BlockSpec((tm, 512), lambda b, i: (b * nt + i, ZP_U)),
            pl.BlockSpec((None, 4, HEAD_W, HEAD_W), lambda b, i: (layer, 0, 0, 0)),
            _resident((1, 512)),
        ],
        out_specs=pl.BlockSpec((tm, 512), lambda b, i: (b * nt + i, 0)),
        out_shape=jax.ShapeDtypeStruct((B * T, 512), BF16),
        scratch_shapes=[pltpu.VMEM((tm + 16, 512), F32)],
        compiler_params=_cparams(("parallel", "arbitrary")),
        name="pool_prompt",
    )(z, w_pool, sc)


def _hgrn_gates(f, log_lb, log1m_lb, one_m_lb):
    ls = jnp.minimum(f, 0.0) - jnp.log(1.0 + jnp.exp(-jnp.abs(f)))
    c = log1m_lb + ls
    mx = jnp.maximum(log_lb, c)
    lg = mx + jnp.log(1.0 + jnp.exp(-jnp.abs(log_lb - c)))
    kk = one_m_lb * _sigmoid(-f)
    return lg, kk


def _hgrn_consts(C):
    r = lax.broadcasted_iota(jnp.int32, (C, C), 0)
    c = lax.broadcasted_iota(jnp.int32, (C, C), 1)
    tri = (r >= c).astype(F32)
    rin = lax.broadcasted_iota(jnp.int32, (C, 1), 0) & (HGRN_SUB - 1)
    validf = [(rin >= d).astype(F32) for d in range(HGRN_SUB)]
    bdiff = (r >> 3) - (c >> 3)
    bandf = [(bdiff == d).astype(F32) for d in range(C // HGRN_SUB)]
    return tri, validf, bandf


def _hgrn_head(q, kk, v, b, ST, consts, C):
    _, validf, bandf = consts
    nb = C // HGRN_SUB
    b3 = b.reshape(nb, HGRN_SUB, HEAD_W)
    kk3 = kk.reshape(nb, HGRN_SUB, HEAD_W)
    v3 = v.reshape(nb, HGRN_SUB, HEAD_W)

    def back(x3, d):
        return pltpu.roll(x3, d, 1).reshape(C, HEAD_W)

    o = jnp.sum(q * kk, axis=-1, keepdims=True) * v
    for d in range(1, HGRN_SUB):
        e = jnp.exp2(jnp.minimum(b - back(b3, d), 0.0))
        a = jnp.sum(q * back(kk3, d) * e, axis=-1, keepdims=True)
        o = o + (a * validf[d]) * back(v3, d)

    bend = b3[:, HGRN_SUB - 1:HGRN_SUB, :]
    bstart = jnp.concatenate([jnp.zeros((1, 1, HEAD_W), F32), bend[:-1]], axis=0)
    qt = (q.reshape(nb, HGRN_SUB, HEAD_W) * jnp.exp2(b3 - bstart)).reshape(C, HEAD_W).astype(BF16)
    a_off = None
    for d in range(1, nb):
        bsh = jnp.concatenate([bend[d - 1:]] + [bend[nb - 1:nb]] * (d - 1), axis=0)
        kd = (kk3 * jnp.exp2(bsh - b3)).reshape(C, HEAD_W).astype(BF16)
        panel = lax.dot_general(qt, kd, (((1,), (1,)), ((), ())), preferred_element_type=F32)
        a_off = panel * bandf[d] if a_off is None else a_off + panel * bandf[d]
    vb = v.astype(BF16)
    o = o + jnp.dot(a_off.astype(BF16), vb, preferred_element_type=F32)

    qe = (q * jnp.exp2(b)).astype(BF16)
    o = o + lax.dot_general(qe, ST.astype(BF16), (((1,), (1,)), ((), ())), preferred_element_type=F32)
    b_end = b[C - 1:C, :]
    ke = (kk * jnp.exp2(b_end - b)).astype(BF16)
    ST_new = ST * jnp.exp2(b_end) + lax.dot_general(vb, ke, (((0,), (0,)), ((), ())),
                                                     preferred_element_type=F32)
    return o, ST_new


def _hgrn_body(q_ref, f_ref, i_ref, g_ref, lbp_ref, gh_ref, o_ref, st_ref, ST_scr, *, tb, C):
    it = pl.program_id(1)

    @pl.when(it == 0)
    def _():
        ST_scr[...] = jnp.zeros(ST_scr.shape, F32)

    lbp = lbp_ref[...]
    gh = gh_ref[...]
    consts = _hgrn_consts(C)

    def chunk_of(heads):
        w = slice(heads[0] * HEAD_W, (heads[-1] + 1) * HEAD_W)

        def chunk(ci, carry):
            rows = pl.ds(pl.multiple_of(ci * C, C), C)
            q = _silu(q_ref[rows, w])
            lg, kk = _hgrn_gates(f_ref[rows, w], lbp[0:1, w], lbp[1:2, w], lbp[2:3, w])
            b = jnp.dot(consts[0], lg * LOG2E, preferred_element_type=F32, precision=lax.Precision.HIGHEST)
            v = i_ref[rows, w]
            gate = _silu(g_ref[rows, w])
            for n, h in enumerate(heads):
                sl = slice(n * HEAD_W, (n + 1) * HEAD_W)
                o, ST_new = _hgrn_head(q[:, sl], kk[:, sl], v[:, sl], b[:, sl], ST_scr[h], consts, C)
                ST_scr[h] = ST_new
                ms = jnp.mean(o * o, axis=-1, keepdims=True)
                o_ref[rows, h * HEAD_W:(h + 1) * HEAD_W] = (
                    o * lax.rsqrt(ms + EPS) * gh * gate[:, sl]).astype(o_ref.dtype)
            return carry

        return chunk

    for heads in HGRN_HEAD_GROUPS:
        lax.fori_loop(0, tb // C, chunk_of(heads), 0, unroll=HGRN_UNROLL)

    @pl.when(it == pl.num_programs(1) - 1)
    def _():
        for h in range(A_HEADS):
            st_ref[h] = ST_scr[h].T


def _hgrn_prompt(z, lbp, gh, B, T, tb=512, C=HGRN_CHUNK):
    nt = T // tb

    def zspec(cb):
        return pl.BlockSpec((tb, 512), lambda b, i: (b * nt + i, cb))

    return pl.pallas_call(
        functools.partial(_hgrn_body, tb=tb, C=C),
        grid=(B, nt),
        in_specs=[zspec(ZP_HQ), zspec(ZP_HF), zspec(ZP_HI), zspec(ZP_HG),
                  _resident((3, 512)), _resident((1, HEAD_W))],
        out_specs=[
            pl.BlockSpec((tb, 512), lambda b, i: (b * nt + i, 0)),
            pl.BlockSpec((None, A_HEADS, HEAD_W, HEAD_W), lambda b, i: (b, 0, 0, 0)),
        ],
        out_shape=[
            jax.ShapeDtypeStruct((B * T, 512), BF16),
            jax.ShapeDtypeStruct((B, A_HEADS, HEAD_W, HEAD_W), F32),
        ],
        scratch_shapes=[pltpu.VMEM((A_HEADS, HEAD_W, HEAD_W), F32)],
        compiler_params=_cparams(("parallel", "arbitrary")),
        name="hgrn_prompt",
    )(z, z, z, z, lbp, gh)


def _merge_body(x_ref, a_ref, p_ref, o_ref, g0_ref, g1_ref, g2_ref, wb_ref, wo_ref, y_ref):
    merged = None
    for n, (br, gz) in enumerate(((a_ref, g0_ref), (p_ref, g1_ref), (o_ref, g2_ref))):
        proj = jnp.dot(br[...].astype(BF16), wb_ref[n], preferred_element_type=F32)
        term = _sigmoid(gz[...]) * proj
        merged = term if merged is None else merged + term
    y_ref[...] = x_ref[...] + jnp.dot(merged.astype(BF16), wo_ref[...], preferred_element_type=F32)


def _merge(x, a, p, o, zg, wb, wo, layer, tm):
    M, D = x.shape

    def gspec(n):
        return pl.BlockSpec((tm, D), lambda i: (i, n))

    bspec = pl.BlockSpec((tm, BRANCH_W), lambda i: (i, 0))
    return pl.pallas_call(
        _merge_body,
        grid=(M // tm,),
        in_specs=[pl.BlockSpec((tm, D), lambda i: (i, 0)), bspec, bspec, bspec,
                  gspec(0), gspec(1), gspec(2),
                  pl.BlockSpec((None, N_BRANCH, BRANCH_W, D), lambda i: (layer, 0, 0, 0)),
                  pl.BlockSpec((None, D, D), lambda i: (layer, 0, 0))],
        out_specs=pl.BlockSpec((tm, D), lambda i: (i, 0)),
        out_shape=jax.ShapeDtypeStruct((M, D), F32),
        compiler_params=_cparams(("parallel",)),
        name="merge",
    )(x, a, p, o, zg, zg, zg, wb, wo)


def _mlp_body(x_ref, g_ref, wu_ref, wd_ref, y_ref, *, fc):
    x = x_ref[...]
    ms = jnp.mean(x * x, axis=-1, keepdims=True)
    hm = (x * lax.rsqrt(ms + EPS) * g_ref[...]).astype(BF16)
    acc = x
    for c in range(wu_ref.shape[1] // fc):
        u = jnp.dot(hm, wu_ref[:, c * fc:(c + 1) * fc], preferred_element_type=F32)
        r = jnp.maximum(u, 0.0)
        acc = acc + jnp.dot((r * r).astype(BF16), wd_ref[c * fc:(c + 1) * fc, :], preferred_element_type=F32)
    y_ref[...] = acc


def _mlp(x, g, wu, wd, layer, tm, fc=1024):
    M, D = x.shape
    Fd = wu.shape[2]
    return pl.pallas_call(
        functools.partial(_mlp_body, fc=fc),
        grid=(M // tm,),
        in_specs=[pl.BlockSpec((tm, D), lambda i: (i, 0)), _resident((1, D)),
                  pl.BlockSpec((None, D, Fd), lambda i: (layer, 0, 0), pipeline_mode=pl.Buffered(1)),
                  pl.BlockSpec((None, Fd, D), lambda i: (layer, 0, 0), pipeline_mode=pl.Buffered(1))],
        out_specs=pl.BlockSpec((tm, D), lambda i: (i, 0)),
        out_shape=jax.ShapeDtypeStruct((M, D), F32),
        compiler_params=_cparams(("parallel",)),
        name="mlp",
    )(x, g, wu, wd)


def _sprep_body(q_ref, k_ref, gq_ref, gk_ref, qn_ref, kT_ref, ss_ref):
    qnT = _group_norm_T(q_ref[...].T, gq_ref[...]) * (A_QK_DIM ** -0.5)
    knT = _group_norm_T(k_ref[...].T, gk_ref[...])
    qn_ref[...] = qnT
    kT_ref[...] = knT
    ss_ref[...] = jnp.sum((qnT * knT).reshape(8, A_QK_DIM, qnT.shape[1]), axis=1)


def _sprep(zs, gq_col, gk_col):
    Bs = zs.shape[0]
    return pl.pallas_call(
        _sprep_body,
        grid=(1,),
        in_specs=[pl.BlockSpec((Bs, 512), lambda i: (0, ZC_Q)),
                  pl.BlockSpec((Bs, 512), lambda i: (0, ZC_K)),
                  _resident((512, 1)), _resident((512, 1))],
        out_specs=[_resident((512, Bs)), _resident((512, Bs)), _resident((8, Bs))],
        out_shape=[jax.ShapeDtypeStruct((512, Bs), F32),
                   jax.ShapeDtypeStruct((512, Bs), F32),
                   jax.ShapeDtypeStruct((8, Bs), F32)],
        compiler_params=_cparams(("arbitrary",)),
        name="sample_prep",
    )(zs, zs, gq_col, gk_col)


def _decode_body(pt_ref, lam_ref, qn_ref, ss_ref, v_ref, bias_ref, gs_ref, *rest, npages, out_scale):
    del pt_ref
    _decode_seq(pl.program_id(0), lam_ref[0], qn_ref, ss_ref, v_ref, bias_ref, gs_ref,
                rest[:npages], rest[npages:2 * npages], rest[2 * npages], out_scale)


def _decode_seq(b, lam, qn_ref, ss_ref, v_ref, bias_ref, gs_ref, k_refs, v_refs, o_ref, out_scale):
    npages = len(k_refs)
    Bs = qn_ref.shape[1]

    lane_q = lax.broadcasted_iota(jnp.int32, qn_ref.shape, 1)
    q_col = jnp.sum(jnp.where(lane_q == b, qn_ref[...], 0.0), axis=-1, keepdims=True)
    q_b = jnp.broadcast_to(q_col, (8 * A_QK_DIM, PAGE))
    s_pages = [jnp.sum((k_refs[p][...] * q_b).reshape(8, A_QK_DIM, PAGE), axis=1)
               for p in range(npages)]
    s_pages[-1] = s_pages[-1] + bias_ref[:, 0:PAGE]
    lane_b = lax.broadcasted_iota(jnp.int32, (8, Bs), 1)
    s_self = (jnp.sum(jnp.where(lane_b == b, ss_ref[...], 0.0), axis=-1, keepdims=True)
              + bias_ref[:, PAGE:PAGE + 1])

    m = s_self
    for s in s_pages:
        m = jnp.maximum(m, jnp.max(s, axis=-1, keepdims=True))
    p_self = jnp.exp(s_self - m)
    p_pages = [jnp.exp(s - m) for s in s_pages]
    l = p_self
    for p in p_pages:
        l = l + jnp.sum(p, axis=-1, keepdims=True)
    row1 = lax.broadcasted_iota(jnp.int32, (8, 1), 0)
    coef = jnp.where((row1 & 1) == 0, 1.0, -lam) / l

    def pair(x):
        return x + pltpu.roll(x, 7, 0)

    w_self = pair(jnp.broadcast_to(p_self * coef, (8, HEAD_W)))
    w_all = jnp.concatenate([pair(p_pages[pg] * coef) for pg in range(npages)], axis=0)
    rr = lax.broadcasted_iota(jnp.int32, (PAGE, A_HEADS * PAGE), 0)
    cc = lax.broadcasted_iota(jnp.int32, (PAGE, A_HEADS * PAGE), 1)
    spread = ((cc >> 2) == rr).astype(BF16)
    w_exp = jnp.dot(w_all.astype(BF16), spread, preferred_element_type=F32)
    row_h = lax.broadcasted_iota(jnp.int32, (8, A_HEADS * PAGE), 0) >> 1
    own = (lax.broadcasted_iota(jnp.int32, (8, A_HEADS * PAGE), 1) & (A_HEADS - 1)) == row_h
    acc = jnp.zeros((8, HEAD_W), F32)
    for pg in range(npages):
        wp = jnp.where(own, w_exp[8 * pg:8 * pg + 8, :], 0.0).astype(BF16)
        acc = acc + jnp.dot(wp, v_refs[pg][...].astype(BF16), preferred_element_type=F32)
    v_new = v_ref[pl.ds(b, 1), :]
    outs = []
    for h in range(A_HEADS):
        oh = acc[2 * h:2 * h + 1, :] + w_self[2 * h:2 * h + 1, :] * v_new[:, h * HEAD_W:(h + 1) * HEAD_W]
        ms = jnp.mean(oh * oh, axis=-1, keepdims=True)
        outs.append(oh * lax.rsqrt(ms + EPS) * gs_ref[...] * out_scale)
    o_ref[pl.ds(b, 1), :] = jnp.concatenate(outs, axis=1)


def _decode(pt_flat, lam, qn, ss, zs, bias_dec, gs, ckT, cv, layer, npages, out_scale):
    Bs = qn.shape[1]

    def page_spec(p):
        return pl.BlockSpec((None, None, 512, PAGE),
                            lambda b, pt, p=p: (layer, pt[b * npages + p], 0, 0))

    in_specs = [
        pl.BlockSpec(memory_space=pltpu.SMEM),
        pl.BlockSpec((512, Bs), lambda b, pt: (0, 0)),
        pl.BlockSpec((8, Bs), lambda b, pt: (0, 0)),
        pl.BlockSpec((Bs, 512), lambda b, pt: (0, ZC_V)),
        pl.BlockSpec((8, 2 * PAGE), lambda b, pt: (0, 0)),
        pl.BlockSpec((1, HEAD_W), lambda b, pt: (0, 0)),
    ] + [page_spec(p) for p in range(npages)] * 2
    return pl.pallas_call(
        functools.partial(_decode_body, npages=npages, out_scale=out_scale),
        grid_spec=pltpu.PrefetchScalarGridSpec(
            num_scalar_prefetch=1,
            grid=(Bs,),
            in_specs=in_specs,
            out_specs=pl.BlockSpec((Bs, 512), lambda b, pt: (0, 0)),
        ),
        out_shape=jax.ShapeDtypeStruct((Bs, 512), F32),
        compiler_params=_cparams(("arbitrary",)),
        name="decode_attn",
    )(pt_flat, lam, qn, ss, zs, bias_dec, gs, *([ckT] * npages), *([cv] * npages))


def _smix_body(q_ref, f_ref, i_ref, g_ref, u_ref, sp_ref, s0_ref, lbp_ref, gh_ref, wp_ref, sc_ref, s1_in,
               o_ref, p_ref, spn_ref, s1_ref, o_scr, *, ns, past_len):
    del s1_in
    q = _silu(q_ref[...])
    lg, kk = _hgrn_gates(f_ref[...], lbp_ref[0:1, :], lbp_ref[1:2, :], lbp_ref[2:3, :])
    g = jnp.exp(lg)
    v = i_ref[...]
    eye = (lax.broadcasted_iota(jnp.int32, (HEAD_W, HEAD_W), 0)
           == lax.broadcasted_iota(jnp.int32, (HEAD_W, HEAD_W), 1))

    def col_of(row):
        return jnp.sum(jnp.where(eye, jnp.broadcast_to(row, eye.shape), 0.0), axis=-1, keepdims=True)

    for s in range(ns):
        for h in range(A_HEADS):
            sl = slice(h * HEAD_W, (h + 1) * HEAD_W)
            S_new = (col_of(g[s:s + 1, sl]) * s0_ref[s, h]
                     + col_of(kk[s:s + 1, sl]) * v[s:s + 1, sl])
            s1_ref[s, h] = S_new
            o_scr[s:s + 1, sl] = jnp.sum(col_of(q[s:s + 1, sl]) * S_new, axis=0, keepdims=True)
    zg = g_ref[...]
    for h in range(A_HEADS):
        sl = slice(h * HEAD_W, (h + 1) * HEAD_W)
        oh = o_scr[:, sl]
        ms = jnp.mean(oh * oh, axis=-1, keepdims=True)
        o_ref[:, sl] = oh * lax.rsqrt(ms + EPS) * gh_ref[...] * _silu(zg[:, sl])

    u = u_ref[...]
    for g_i, win in enumerate(P_WINDOWS):
        sl = slice(g_i * HEAD_W, (g_i + 1) * HEAD_W)
        ug = u[:, sl]
        sacc = ug
        for j in range(1, win):
            sacc = sacc + sp_ref[POOL_BUF - j][:, sl]
        cnt = float(min(past_len + 1, win))
        pooled = sacc / cnt - ug
        p_ref[:, sl] = (jnp.dot(pooled.astype(BF16), wp_ref[g_i], preferred_element_type=F32)
                        * sc_ref[:, sl])
    for j in range(POOL_BUF - 1):
        spn_ref[j] = sp_ref[j + 1]
    spn_ref[POOL_BUF - 1] = u


def _smix(zs, sp_t, s0, lbp, gh, w_pool, sc, s1_all, layer, past_len, ns=8):
    Bs = zs.shape[0]

    def zspec(cb):
        return pl.BlockSpec((ns, 512), lambda i: (i, cb))

    return pl.pallas_call(
        functools.partial(_smix_body, ns=ns, past_len=past_len),
        grid=(Bs // ns,),
        in_specs=[zspec(ZC_HQ), zspec(ZC_HF), zspec(ZC_HI), zspec(ZC_HG), zspec(ZC_U),
                  pl.BlockSpec((None, POOL_BUF, ns, 512), lambda i: (layer, 0, i, 0)),
                  pl.BlockSpec((None, ns, A_HEADS, HEAD_W, HEAD_W), lambda i: (layer, i, 0, 0, 0)),
                  _resident((3, 512)), _resident((1, HEAD_W)),
                  pl.BlockSpec((None, 4, HEAD_W, HEAD_W), lambda i: (layer, 0, 0, 0)),
                  _resident((1, 512)),
                  pl.BlockSpec(memory_space=pl.ANY)],
        out_specs=[pl.BlockSpec((ns, 512), lambda i: (i, 0)),
                   pl.BlockSpec((ns, 512), lambda i: (i, 0)),
                   pl.BlockSpec((POOL_BUF, ns, 512), lambda i: (0, i, 0)),
                   pl.BlockSpec((None, ns, A_HEADS, HEAD_W, HEAD_W), lambda i: (layer, i, 0, 0, 0))],
        out_shape=[jax.ShapeDtypeStruct((Bs, 512), F32),
                   jax.ShapeDtypeStruct((Bs, 512), F32),
                   jax.ShapeDtypeStruct((POOL_BUF, Bs, 512), F32),
                   jax.ShapeDtypeStruct(s1_all.shape, F32)],
        input_output_aliases={11: 3},
        scratch_shapes=[pltpu.VMEM((ns, 512), F32)],
        compiler_params=_cparams(("parallel",)),
        name="sample_mix",
    )(zs, zs, zs, zs, zs, sp_t, s0, lbp, gh, w_pool, sc, s1_all)


def kernel(x_prompt, x_sample, cache_k, cache_v, state_pool, state_hgrn, page_table, rel_table, lb_param, w_in, g_mix, g_q, g_k, lam_p, g_sub, w_pool, pool_scale, g_h, w_branch, w_out, g_mlp, w_up, w_down):
    B, T, D = x_prompt.shape
    Bs = x_sample.shape[0]
    depth, n_phys = cache_k.shape[:2]
    npages = page_table.shape[1]
    past_len = npages * PAGE
    tile = ATT_TILE
    att_steps = B * A_HEADS * (T // tile)
    spg = Bs // att_steps if Bs % att_steps == 0 else 0

    lb_all = jnp.cumsum(jax.nn.softmax(lb_param.astype(F32), axis=0), axis=0)
    lb_all = lb_all - lb_all[:1]
    lbp_all = jnp.stack([jnp.log(lb_all), jnp.log1p(-lb_all), 1.0 - lb_all], axis=1)
    lp = lam_p.astype(F32)
    lam_dyn = jnp.exp(jnp.sum(lp[:, 0] * lp[:, 1], axis=-1)) - jnp.exp(jnp.sum(lp[:, 2] * lp[:, 3], axis=-1))
    w_mix_b = w_in[:, :, :MIX_W].astype(BF16)
    w_gate_b = w_in[:, :, MIX_W:].astype(BF16)
    w_pool_b, w_branch_b = w_pool.astype(BF16), w_branch.astype(BF16)
    w_out_b, w_up_b, w_down_b = w_out.astype(BF16), w_up.astype(BF16), w_down.astype(BF16)

    rr = np.arange(tile)[:, None]
    cc = np.arange(2 * tile)[None, :]
    dist = rr + tile - cc
    bkt_prompt = np.where(dist >= 0, _rel_bucket_np(np.maximum(dist, 0)), -1).astype(np.int32)
    bias_prompt = _bias_tiles(rel_table, bkt_prompt, LOG2E)
    dd = np.concatenate([PAGE - np.arange(PAGE), np.zeros(PAGE, np.int64)])
    bkt_dec = np.broadcast_to(_rel_bucket_np(dd)[None, :], (8, 2 * PAGE)).astype(np.int32)
    bias_dec_h = _bias_tiles(rel_table, bkt_dec, 1.0)
    bias_dec = jnp.repeat(bias_dec_h[:, 0, :], 2, axis=0)

    tab2 = (rel_table.astype(F32) - rel_table[REL_BUCKETS - 1:].astype(F32)) * LOG2E
    bias_hi = jnp.maximum(jnp.max(tab2, axis=0), 0.0)
    bias_lo = jnp.minimum(jnp.min(tab2, axis=0), 0.0)
    slack = 1.01
    k_bound = 8.0 * slack * jnp.max(jnp.abs(g_k.astype(F32)), axis=-1)
    q_bound = 8.0 * slack * (A_QK_DIM ** -0.5 * LOG2E) * jnp.max(jnp.abs(g_q.astype(F32)), axis=-1)
    spread = 2.0 * q_bound * k_bound + jnp.max(bias_hi - bias_lo)
    bounded = (spread <= ATT_MAX_SPREAD).astype(F32)

    ckT = jnp.transpose(cache_k, (0, 1, 3, 4, 5, 2)).reshape(depth, n_phys, 512, PAGE)
    cv4 = cache_v.reshape(depth, n_phys, A_HEADS * PAGE, HEAD_W)
    sp_t = jnp.transpose(state_pool, (0, 2, 1, 3))
    pt_flat = page_table.reshape(-1).astype(jnp.int32)

    xp = x_prompt.reshape(B * T, D)
    xs = x_sample.reshape(Bs, D)
    ks_l, vs_l, pp_l, ps_l, sp_l = [], [], [], [], []
    kT_all = jnp.zeros((depth, B, 512, T), F32)
    v4_all = jnp.zeros((depth, B, A_HEADS * T, HEAD_W), F32)
    s1_all = jnp.zeros((depth, Bs, A_HEADS, HEAD_W, HEAD_W), F32)
    for l in range(depth):
        lam_init = 0.8 - 0.6 * math.exp(-0.3 * l)
        out_scale = 1.0 - lam_init
        lam = (lam_dyn[l] + lam_init).reshape(1).astype(F32)
        par = jnp.concatenate([lam, bounded[l].reshape(1), k_bound[l].reshape(1), bias_hi]).astype(F32)
        gmix = g_mix[l].reshape(1, D)
        gq_col = jnp.tile(g_q[l], 8).reshape(512, 1)
        gk_col = jnp.tile(g_k[l], 8).reshape(512, 1)
        gq2 = jnp.tile(g_q[l], 2).reshape(1, HEAD_W)
        gs = g_sub[l].reshape(1, HEAD_W)
        gh = g_h[l].reshape(1, HEAD_W)
        sc = pool_scale[l].reshape(1, 512)
        gmlp = g_mlp[l].reshape(1, D)
        lbp = lbp_all[l]

        zs = _inproj(xs, gmix, w_mix_b, l, tm=Bs)
        zgs = _inproj(xs, gmix, w_gate_b, l, tm=Bs)
        qn, kTs, ssf = _sprep(zs, gq_col, gk_col)
        z, kT_all, kTb, v4_all, vb1 = _inproj_prompt(xp, gmix, w_mix_b, gk_col, kT_all, v4_all, l, B, T)
        zg = _inproj(xp, gmix, w_gate_b, l, tm=512)

        a, a_s = _attn_prompt(pt_flat, par, z, kTb, vb1, bias_prompt, gq2, gs, qn, ssf, zs, bias_dec,
                              ckT, cv4, l, B, T, out_scale, npages, spg)
        if spg == 0:
            a_s = _decode(pt_flat, lam, qn, ssf, zs, bias_dec, gs, ckT, cv4, l, npages, out_scale)

        p = _pool_prompt(z, w_pool_b, sc, l, B, T)
        o, st = _hgrn_prompt(z, lbp, gh, B, T)
        x1 = _merge(xp, a, p, o, zg, w_branch_b, w_out_b, l, tm=512)
        xp = _mlp(x1, gmlp, w_up_b, w_down_b, l, tm=512)
        pp_l.append(z.reshape(B, T, -1)[:, T - POOL_BUF:, ZP_U * 512:(ZP_U + 1) * 512])
        sp_l.append(st)

        o_s, p_s, spn, s1_all = _smix(zs, sp_t, state_hgrn, lbp, gh, w_pool_b, sc, s1_all, l, past_len)
        x1s = _merge(xs, a_s, p_s, o_s, zgs, w_branch_b, w_out_b, l, tm=Bs)
        xs = _mlp(x1s, gmlp, w_up_b, w_down_b, l, tm=Bs)
        ks_l.append(kTs)
        vs_l.append(zs[:, ZC_V * 512:(ZC_V + 1) * 512])
        ps_l.append(spn)

    k_prompt = jnp.transpose(kT_all.reshape(depth, B, A_HEADS, 2, A_QK_DIM, T), (0, 1, 5, 2, 3, 4))
    v_prompt = v4_all.reshape(depth, B, T, A_HEADS, A_V_DIM)
    k_sample = jnp.transpose(jnp.stack(ks_l).reshape(depth, A_HEADS, 2, A_QK_DIM, Bs), (0, 4, 1, 2, 3))[:, :, None]
    v_sample = jnp.stack(vs_l).reshape(depth, Bs, 1, A_HEADS, A_V_DIM)
    pool_prompt = jnp.stack(pp_l)
    pool_sample = jnp.transpose(jnp.stack(ps_l), (0, 2, 1, 3))
    return (xp.reshape(B, T, D), xs.reshape(Bs, 1, D), k_prompt, v_prompt, k_sample, v_sample,
            pool_prompt, pool_sample, jnp.stack(sp_l), s1_all)
```
